```python
import math
import jax, jax.numpy as jnp
from jax import lax
import numpy as np

D_MODEL = 4096
BATCH = 2
SEQ = 4096
DEPTH = 2

HEAD_DIM = 128
D_NSA = D_MODEL // 2
NSA_HEADS = D_NSA // HEAD_DIM
NSA_KV_HEADS = 2
NSA_GROUP = NSA_HEADS // NSA_KV_HEADS
D_KV = NSA_KV_HEADS * HEAD_DIM
CMP_LEN = 32
CMP_STRIDE = 16
CMP_HIDDEN = 256
SEL_LEN = 64
SEL_TOPN = 16
WINDOW = 512
Q_BLOCK = 128
D_RET = D_MODEL // 2
RET_HEADS = 8
RET_HEAD_DIM = D_RET // RET_HEADS
RET_CHUNK = 128
D_FF = 4 * D_MODEL
N_BUCKETS = 32
MAX_DISTANCE = 128
ROPE_BASE = 10000.0
EPS = 1e-6
NEG_INF = -1e30
FORCE_SCORE = 1e9

SPLIT_SIZES = (D_NSA, D_KV, D_KV, D_KV, D_KV, D_KV, D_KV, 3 * NSA_HEADS,
               D_RET, D_RET, D_RET, D_RET, D_MODEL, D_MODEL)
D_IN = D_NSA + 6 * D_KV + 3 * NSA_HEADS + 4 * D_RET + 2 * D_MODEL

kernel_name = "hybrid_nsa_retention_gated_block"


def rms_norm(x, g):
    x32 = x.astype(jnp.float32)
    y = x32 * lax.rsqrt(jnp.mean(x32 * x32, axis=-1, keepdims=True) + EPS)
    return (y * g.astype(jnp.float32)).astype(x.dtype)


def masked_softmax(logits, mask):
    logits = jnp.where(mask, logits.astype(jnp.float32), NEG_INF)
    m = jnp.max(logits, axis=-1, keepdims=True)
    e = jnp.where(mask, jnp.exp(logits - m), 0.0)
    return e / jnp.maximum(jnp.sum(e, axis=-1, keepdims=True), 1e-30)


def t5_bucket(dist):
    n = jnp.maximum(dist, 0)
    max_exact = N_BUCKETS // 2
    large = max_exact + (jnp.log(jnp.maximum(n, 1).astype(jnp.float32) / max_exact)
                         / math.log(MAX_DISTANCE / max_exact) * (N_BUCKETS - max_exact)).astype(jnp.int32)
    large = jnp.minimum(large, N_BUCKETS - 1)
    return jnp.where(n < max_exact, n, large)


def rope(x, pos):
    half = x.shape[-1] // 2
    freqs = ROPE_BASE ** (-jnp.arange(half, dtype=jnp.float32) / half)
    ang = pos[:, None] * freqs[None, :]
    cos, sin = jnp.cos(ang), jnp.sin(ang)
    x1, x2 = x[..., :half], x[..., half:]
    return jnp.concatenate([x1 * cos - x2 * sin, x2 * cos + x1 * sin], axis=-1)


def compress_blocks(t, pos, w1, w2):
    B, T, G, d = t.shape
    chunks = t.reshape(B, T // CMP_STRIDE, CMP_STRIDE, G, d)
    blocks = jnp.concatenate([chunks[:, :-1], chunks[:, 1:]], axis=2)
    blocks = blocks + pos[None, None, :, None, :]
    blocks = blocks.transpose(0, 3, 1, 2, 4).reshape(B, G, -1, CMP_LEN * d)
    return jax.nn.gelu(blocks @ w1) @ w2


def nsa_attention(q, k_cmp, v_cmp, k_sel, v_sel, k_win, v_win, gate_logits,
                  q_norm_g, k_norm_g, cmp_pos, cmp_w1, cmp_w2, rel_bias):
    B, T, _ = q.shape
    G, HG, d = NSA_KV_HEADS, NSA_GROUP, HEAD_DIM
    nb = T // Q_BLOCK
    n_c = T // CMP_STRIDE - 1
    n_s = T // SEL_LEN
    top_n = min(SEL_TOPN, n_s)
    scale = d ** -0.5

    def heads(t, h):
        return t.reshape(B, T, h, d)

    qh = rms_norm(heads(q, NSA_HEADS), q_norm_g)
    kc = rms_norm(compress_blocks(heads(k_cmp, G), cmp_pos[0], cmp_w1[0], cmp_w2[0]), k_norm_g[0])
    vc = compress_blocks(heads(v_cmp, G), cmp_pos[1], cmp_w1[1], cmp_w2[1])
    ks = rms_norm(heads(k_sel, G), k_norm_g[1]).transpose(0, 2, 1, 3).reshape(B, G, n_s, SEL_LEN, d)
    vs = heads(v_sel, G).transpose(0, 2, 1, 3).reshape(B, G, n_s, SEL_LEN, d)
    pad = ((0, 0), (0, 0), (WINDOW, 0), (0, 0))
    kw = jnp.pad(rms_norm(heads(k_win, G), k_norm_g[2]).transpose(0, 2, 1, 3), pad)
    vw = jnp.pad(heads(v_win, G).transpose(0, 2, 1, 3), pad)

    q_blocks = qh.reshape(B, nb, Q_BLOCK, G, HG, d).transpose(1, 0, 3, 4, 2, 5)
    g_blocks = jax.nn.sigmoid(gate_logits).reshape(B, nb, Q_BLOCK, G, HG, 3).transpose(1, 0, 3, 4, 2, 5)

    c_start = jnp.arange(n_c) * CMP_STRIDE
    c_end = c_start + CMP_LEN - 1
    s_start = jnp.arange(n_s) * SEL_LEN
    overlap = ((c_start[:, None] < s_start[None, :] + SEL_LEN)
               & (c_end[:, None] >= s_start[None, :])).astype(jnp.float32)
    rel_gb = rel_bias.reshape(N_BUCKETS, G, HG).transpose(1, 0, 2)
    b_idx = jnp.arange(B)[:, None, None, None]
    g_idx = jnp.arange(G)[None, :, None, None]
    blk = jnp.arange(n_s)

    def block(args):
        qb, gb, i = args
        t = i * Q_BLOCK + jnp.arange(Q_BLOCK)
        dist_c = t[:, None] - c_end[None, :]
        bias_c = rel_bias[t5_bucket(dist_c)].transpose(2, 0, 1).reshape(G, HG, Q_BLOCK, n_c)
        logit_c = jnp.einsum('bghqd,bgcd->bghqc', qb, kc).astype(jnp.float32) * scale + bias_c
        p_c = masked_softmax(logit_c, dist_c >= 0)
        o_c = jnp.einsum('bghqc,bgcd->bghqd', p_c.astype(vc.dtype), vc)
        imp = jnp.einsum('bghqc,cs->bgqs', p_c, overlap)
        cur = t // SEL_LEN
        forced = (blk[None] == 0) | (blk[None] == cur[:, None]) | (blk[None] == cur[:, None] - 1)
        future = blk[None] > cur[:, None]
        score = jnp.where(forced, FORCE_SCORE, jnp.where(future, NEG_INF, imp))
        _, idx = lax.top_k(score, top_n)
        k_g = ks[b_idx, g_idx, idx]
        v_g = vs[b_idx, g_idx, idx]
        tok = idx[..., None] * SEL_LEN + jnp.arange(SEL_LEN)
        dist_s = t[None, None, :, None, None] - tok
        bias_s = jnp.moveaxis(rel_gb[g_idx[..., None], t5_bucket(dist_s)], -1, 2)
        logit_s = jnp.einsum('bghqd,bgqnld->bghqnl', qb, k_g).astype(jnp.float32) * scale + bias_s
        mask_s = (dist_s >= 0)[:, :, None].reshape(B, G, 1, Q_BLOCK, -1)
        p_s = masked_softmax(logit_s.reshape(B, G, HG, Q_BLOCK, -1), mask_s).reshape(logit_s.shape)
        o_s = jnp.einsum('bghqnl,bgqnld->bghqd', p_s.astype(v_g.dtype), v_g)
        k_w = lax.dynamic_slice_in_dim(kw, i * Q_BLOCK, WINDOW + Q_BLOCK, axis=2)
        v_w = lax.dynamic_slice_in_dim(vw, i * Q_BLOCK, WINDOW + Q_BLOCK, axis=2)
        s = i * Q_BLOCK - WINDOW + jnp.arange(WINDOW + Q_BLOCK)
        dist_w = t[:, None] - s[None, :]
        mask_w = (dist_w >= 0) & (dist_w < WINDOW) & (s[None, :] >= 0)
        bias_w = rel_bias[t5_bucket(dist_w)].transpose(2, 0, 1).reshape(G, HG, Q_BLOCK, -1)
        logit_w = jnp.einsum('bghqd,bgkd->bghqk', qb, k_w).astype(jnp.float32) * scale + bias_w
        p_w = masked_softmax(logit_w, mask_w)
        o_w = jnp.einsum('bghqk,bgkd->bghqd', p_w.astype(v_w.dtype), v_w)
        return gb[..., 0:1] * o_c + gb[..., 1:2] * o_s + gb[..., 2:3] * o_w

    out = lax.map(block, (q_blocks, g_blocks, jnp.arange(nb)))
    return out.transpose(1, 0, 4, 2, 3, 5).reshape(B, T, D_NSA)


def retention(q, k, v, g, gn_g):
    B, T, _ = q.shape
    H, d, C = RET_HEADS, RET_HEAD_DIM, RET_CHUNK
    nc = T // C
    pos = jnp.arange(T, dtype=jnp.float32)

    def heads(t):
        return t.astype(jnp.float32).reshape(B, T, H, d).transpose(0, 2, 1, 3)

    qh = rope(heads(q), pos)
    kh = rope(heads(k), pos) * d ** -0.5
    vh = heads(v)
    log_gamma = jnp.log(1.0 - 2.0 ** (-5.0 - jnp.arange(H, dtype=jnp.float32)))
    n = jnp.arange(C, dtype=jnp.float32)
    diff = n[:, None] - n[None, :]
    decay_in = jnp.where(diff >= 0, jnp.exp(jnp.maximum(diff, 0.0)[None] * log_gamma[:, None, None]), 0.0)
    q_decay = jnp.exp((n + 1.0)[None] * log_gamma[:, None])
    k_decay = jnp.exp((C - 1.0 - n)[None] * log_gamma[:, None])
    chunk_decay = jnp.exp(C * log_gamma)

    def chunks(t):
        return t.reshape(B, H, nc, C, d).transpose(2, 0, 1, 3, 4)

    def step(state, xs):
        qc, kc, vc = xs
        inner = jnp.einsum('bhnd,bhmd->bhnm', qc, kc) * decay_in
        o = (jnp.einsum('bhnm,bhmd->bhnd', inner, vc)
             + jnp.einsum('bhnd,bhde->bhne', qc, state) * q_decay[..., None])
        state = state * chunk_decay[:, None, None] + jnp.einsum('bhmd,bhme->bhde', kc * k_decay[..., None], vc)
        return state, o

    state0 = jnp.zeros((B, H, d, d), jnp.float32)
    _, o = lax.scan(step, state0, (chunks(qh), chunks(kh), chunks(vh)))
    o = o.transpose(1, 2, 0, 3, 4).reshape(B, H, T, d)
    mu = jnp.mean(o, axis=-1, keepdims=True)
    var = jnp.mean(jnp.square(o - mu), axis=-1, keepdims=True)
    o = ((o - mu) * lax.rsqrt(var + EPS)).transpose(0, 2, 1, 3).reshape(B, T, D_RET)
    o = o * gn_g.astype(jnp.float32)
    return (jax.nn.silu(g.astype(jnp.float32)) * o).astype(q.dtype)


def setup_inputs(seed: int = 0) -> dict:
    key = jax.random.key(seed)
    ks = jax.random.split(key, 16)
    f32 = jnp.float32

    def nrm(k, shape, scale):
        return jax.random.normal(k, shape, f32) * scale

    def gain(k, shape):
        return 1.0 + 0.02 * jax.random.normal(k, shape, f32)

    return {
        "x": jax.random.normal(ks[0], (BATCH, SEQ, D_MODEL), f32),
        "norm1_g": gain(ks[1], (DEPTH, D_MODEL)),
        "w_in": nrm(ks[2], (DEPTH, D_MODEL, D_IN), D_MODEL ** -0.5),
        "nsa_q_norm_g": gain(ks[3], (DEPTH, HEAD_DIM)),
        "nsa_k_norm_g": gain(ks[4], (DEPTH, 3, HEAD_DIM)),
        "cmp_pos": nrm(ks[5], (DEPTH, 2, CMP_LEN, HEAD_DIM), 0.1),
        "cmp_w1": nrm(ks[6], (DEPTH, 2, CMP_LEN * HEAD_DIM, CMP_HIDDEN), (CMP_LEN * HEAD_DIM) ** -0.5),
        "cmp_w2": nrm(ks[7], (DEPTH, 2, CMP_HIDDEN, HEAD_DIM), CMP_HIDDEN ** -0.5),
        "ret_gn_g": gain(ks[8], (DEPTH, D_RET)),
        "w_up_nsa": nrm(ks[9], (DEPTH, D_NSA, D_MODEL), D_NSA ** -0.5),
        "w_up_ret": nrm(ks[10], (DEPTH, D_RET, D_MODEL), D_RET ** -0.5),
        "w_out": nrm(ks[11], (DEPTH, D_MODEL, D_MODEL), D_MODEL ** -0.5),
        "norm2_g": gain(ks[12], (DEPTH, D_MODEL)),
        "w_ff1": nrm(ks[13], (DEPTH, D_MODEL, D_FF), D_MODEL ** -0.5),
        "w_ff2": nrm(ks[14], (DEPTH, D_FF, D_MODEL), D_FF ** -0.5),
        "rel_bias": nrm(ks[15], (N_BUCKETS, NSA_HEADS), 0.5),
    }


def reference(x, norm1_g, w_in, nsa_q_norm_g, nsa_k_norm_g, cmp_pos, cmp_w1, cmp_w2,
              ret_gn_g, w_up_nsa, w_up_ret, w_out, norm2_g, w_ff1, w_ff2, rel_bias):
    split_idx = [int(s) for s in np.cumsum(SPLIT_SIZES)[:-1]]
    for l in range(DEPTH):
        h = rms_norm(x, norm1_g[l])
        proj = h @ w_in[l]
        (q_a, kc, vc, ksl, vsl, kwn, vwn, gate_a,
         q_r, k_r, v_r, g_r, merge_a, merge_r) = jnp.split(proj, split_idx, axis=-1)
        B, T, _ = x.shape
        o_a = nsa_attention(q_a, kc, vc, ksl, vsl, kwn, vwn, gate_a.reshape(B, T, NSA_HEADS, 3),
                            nsa_q_norm_g[l], nsa_k_norm_g[l], cmp_pos[l], cmp_w1[l], cmp_w2[l], rel_bias)
        o_r = retention(q_r, k_r, v_r, g_r, ret_gn_g[l])
        y = (jax.nn.sigmoid(merge_a) * (o_a @ w_up_nsa[l])
             + jax.nn.sigmoid(merge_r) * (o_r @ w_up_ret[l]))
        x = x + y @ w_out[l]
        h2 = rms_norm(x, norm2_g[l])
        x = x + jnp.square(jax.nn.relu(h2 @ w_ff1[l])) @ w_ff2[l]
    return x
```

```python
import functools
import math

import jax
import jax.numpy as jnp
from jax import lax
from jax.experimental import pallas as pl
from jax.experimental.pallas import tpu as pltpu

F32 = jnp.float32
BF16 = jnp.bfloat16

D_MODEL = 4096
HEAD_DIM = 128
D_NSA = D_MODEL // 2
NSA_HEADS = D_NSA // HEAD_DIM
NSA_KV_HEADS = 2
NSA_GROUP = NSA_HEADS // NSA_KV_HEADS
D_KV = NSA_KV_HEADS * HEAD_DIM
CMP_LEN = 32
CMP_STRIDE = 16
CMP_HIDDEN = 256
SEL_LEN = 64
SEL_TOPN = 16
WINDOW = 512
Q_BLOCK = 128
D_RET = D_MODEL // 2
RET_HEADS = 8
RET_HEAD_DIM = D_RET // RET_HEADS
RET_CHUNK = 128
D_FF = 4 * D_MODEL
N_BUCKETS = 32
MAX_DISTANCE = 128
ROPE_BASE = 10000.0
EPS = 1e-6
NEG_INF = -1e30
FORCE_SCORE = 1e9

COL_Q = 0
COL_KC = D_NSA
COL_KSL = COL_KC + 2 * D_KV
COL_VSL = COL_KSL + D_KV
COL_KWN = COL_VSL + D_KV
COL_VWN = COL_KWN + D_KV
COL_GATE = COL_VWN + D_KV
N_GATE = 3 * NSA_HEADS
COL_RET = 4096
COL_MERGE_A = COL_RET + 4 * D_RET
COL_MERGE_R = COL_MERGE_A + D_MODEL
D_PROJ = COL_MERGE_R + D_MODEL
N_REAL_A = D_NSA + 6 * D_KV + N_GATE

LANES = 128
GQ = NSA_GROUP * Q_BLOCK
WIN_TILES = WINDOW // Q_BLOCK + 1
VMEM_LIMIT = 56 * 1024 * 1024


def _cparams(sem, vmem=VMEM_LIMIT):
    return pltpu.CompilerParams(dimension_semantics=sem, vmem_limit_bytes=vmem)


def _rmsnorm_kernel(x_ref, g_ref, o_ref):
    x = x_ref[...]
    ms = jnp.mean(x * x, axis=-1, keepdims=True)
    o_ref[...] = (x * lax.rsqrt(ms + EPS) * g_ref[...]).astype(o_ref.dtype)


def rmsnorm_bf16(x, g, tm=256):
    n, d = x.shape
    return pl.pallas_call(
        _rmsnorm_kernel,
        grid=(n // tm,),
        in_specs=[pl.BlockSpec((tm, d), lambda i: (i, 0)),
                  pl.BlockSpec((1, d), lambda i: (0, 0))],
        out_specs=pl.BlockSpec((tm, d), lambda i: (i, 0)),
        out_shape=jax.ShapeDtypeStruct((n, d), BF16),
        compiler_params=_cparams(("parallel",)),
        name="rmsnorm",
    )(x, g.reshape(1, d))


def _mm_kernel(a_ref, w_ref, o_ref, *, relu2):
    acc = jnp.dot(a_ref[...], w_ref[...], preferred_element_type=F32)
    if relu2:
        acc = jnp.square(jnp.maximum(acc, 0.0))
    o_ref[...] = acc.astype(o_ref.dtype)


def matmul_bf16(a, w, *, relu2=False, tm=1024, tn=1024, name="mm"):
    m, k = a.shape
    _, n = w.shape
    tm, tn = min(tm, m), min(tn, n)
    return pl.pallas_call(
        functools.partial(_mm_kernel, relu2=relu2),
        grid=(m // tm, n // tn),
        in_specs=[pl.BlockSpec((tm, k), lambda i, j: (i, 0)),
                  pl.BlockSpec((k, tn), lambda i, j: (0, j))],
        out_specs=pl.BlockSpec((tm, tn), lambda i, j: (i, j)),
        out_shape=jax.ShapeDtypeStruct((m, n), BF16),
        compiler_params=_cparams(("parallel", "arbitrary")),
        name=name,
    )(a, w)


def _mm_res_kernel(a_ref, w_ref, r_ref, o_ref):
    k = pl.program_id(2)
    acc = jnp.dot(a_ref[...], w_ref[...], preferred_element_type=F32)

    @pl.when(k == 0)
    def _():
        o_ref[...] = r_ref[...] + acc

    @pl.when(k > 0)
    def _():
        o_ref[...] += acc


def matmul_residual(a, w, res, *, tm=1024, tn=1024, tk=2048, name="mm_res"):
    m, k = a.shape
    _, n = w.shape
    tm, tn, tk = min(tm, m), min(tn, n), min(tk, k)
    return pl.pallas_call(
        _mm_res_kernel,
        grid=(m // tm, n // tn, k // tk),
        in_specs=[pl.BlockSpec((tm, tk), lambda i, j, kk: (i, kk)),
                  pl.BlockSpec((tk, tn), lambda i, j, kk: (kk, j)),
                  pl.BlockSpec((tm, tn), lambda i, j, kk: (i, j))],
        out_specs=pl.BlockSpec((tm, tn), lambda i, j, kk: (i, j)),
        out_shape=jax.ShapeDtypeStruct((m, n), F32),
        compiler_params=_cparams(("parallel", "arbitrary", "arbitrary")),
        name=name,
    )(a, w, res)


def _mm_merge_kernel(a1_ref, w1_ref, a2_ref, w2_ref, g1_ref, g2_ref, o_ref):
    u1 = jnp.dot(a1_ref[...], w1_ref[...], preferred_element_type=F32)
    u2 = jnp.dot(a2_ref[...], w2_ref[...], preferred_element_type=F32)
    y = (jax.nn.sigmoid(g1_ref[...].astype(F32)) * u1
         + jax.nn.sigmoid(g2_ref[...].astype(F32)) * u2)
    o_ref[...] = y.astype(o_ref.dtype)


def merge_up_proj(o_a, w_a, o_r, w_r, proj, *, tm=1024, tn=512):
    m, k = o_a.shape
    n = w_a.shape[1]
    tm = min(tm, m)
    ca, cr = COL_MERGE_A // tn, COL_MERGE_R // tn
    return pl.pallas_call(
        _mm_merge_kernel,
        grid=(m // tm, n // tn),
        in_specs=[pl.BlockSpec((tm, k), lambda i, j: (i, 0)),
                  pl.BlockSpec((k, tn), lambda i, j: (0, j)),
                  pl.BlockSpec((tm, k), lambda i, j: (i, 0)),
                  pl.BlockSpec((k, tn), lambda i, j: (0, j)),
                  pl.BlockSpec((tm, tn), lambda i, j: (i, ca + j)),
                  pl.BlockSpec((tm, tn), lambda i, j: (i, cr + j))],
        out_specs=pl.BlockSpec((tm, tn), lambda i, j: (i, j)),
        out_shape=jax.ShapeDtypeStruct((m, n), BF16),
        compiler_params=_cparams(("parallel", "arbitrary")),
        name="merge_up",
    )(o_a, w_a, o_r, w_r, proj, proj)


def _t5_bucket(dist):
    n = jnp.maximum(dist, 0)
    max_exact = N_BUCKETS // 2
    large = max_exact + (jnp.log(jnp.maximum(n, 1).astype(F32) / max_exact)
                         / math.log(MAX_DISTANCE / max_exact) * (N_BUCKETS - max_exact)).astype(jnp.int32)
    large = jnp.minimum(large, N_BUCKETS - 1)
    return jnp.where(n < max_exact, n, large)


def _bucket_maps():
    kk = jnp.arange(Q_BLOCK)[:, None]
    r = jnp.arange(Q_BLOCK)[None, :]
    maps = []
    for kind in range(WIN_TILES):
        dist = kind * Q_BLOCK + r - kk
        valid = dist >= 0
        if kind == WIN_TILES - 1:
            valid = valid & (dist < WINDOW)
        maps.append(jnp.where(valid, _t5_bucket(dist), -1))
    tile_map = jnp.concatenate(maps, axis=0).astype(jnp.int32)
    mrow = jnp.arange(16)[:, None]
    dist_c = r - CMP_STRIDE * (mrow - 8) - (CMP_LEN - 1)
    cmp_rows = jnp.where(dist_c >= 0, _t5_bucket(dist_c), -1)
    far = jnp.full((8, Q_BLOCK), N_BUCKETS - 1)
    cmp_map = jnp.concatenate([cmp_rows, far], axis=0).astype(jnp.int32)
    return tile_map, cmp_map


def _bias_kernel(relb_ref, tmap_ref, cmap_ref, tb_ref, cb_ref):
    h = pl.program_id(0)

    def lookup(bmap):
        out = jnp.full(bmap.shape, NEG_INF, F32)
        for b in range(N_BUCKETS):
            out = jnp.where(bmap == b, relb_ref[b, h], out)
        return out

    tb_ref[...] = lookup(tmap_ref[...])
    cb_ref[...] = lookup(cmap_ref[...])


def bias_tables(rel_bias):
    tile_map, cmp_map = _bucket_maps()
    rt, rc = tile_map.shape[0], cmp_map.shape[0]
    tb, cb = pl.pallas_call(
        _bias_kernel,
        grid=(NSA_HEADS,),
        in_specs=[pl.BlockSpec(memory_space=pltpu.SMEM),
                  pl.BlockSpec((rt, LANES), lambda h: (0, 0)),
                  pl.BlockSpec((rc, LANES), lambda h: (0, 0))],
        out_specs=[pl.BlockSpec((None, rt, LANES), lambda h: (h // NSA_GROUP, 0, h % NSA_GROUP)),
                   pl.BlockSpec((None, rc, LANES), lambda h: (h // NSA_GROUP, 0, h % NSA_GROUP))],
        out_shape=[jax.ShapeDtypeStruct((NSA_KV_HEADS, rt, GQ), F32),
                   jax.ShapeDtypeStruct((NSA_KV_HEADS, rc, GQ), F32)],
        compiler_params=_cparams(("arbitrary",)),
        name="bias_tables",
    )(rel_bias, tile_map, cmp_map)
    return tb.reshape(NSA_KV_HEADS, WIN_TILES, Q_BLOCK, GQ), cb


def _kv_prep_kernel(ks_ref, vs_ref, kw_ref, vw_ref, g_ref, kso_ref, vso_ref, kwo_ref, vwo_ref):
    def norm(x, g):
        ms = jnp.mean(x * x, axis=-1, keepdims=True)
        return x * lax.rsqrt(ms + EPS) * g

    kso_ref[...] = norm(ks_ref[...].astype(F32), g_ref[1:2, :]).astype(BF16)
    kwo_ref[...] = norm(kw_ref[...].astype(F32), g_ref[2:3, :]).astype(BF16)
    vso_ref[...] = vs_ref[...].astype(F32).T.astype(BF16)
    vwo_ref[...] = vw_ref[...].astype(F32).T.astype(BF16)


def kv_prep(proj, k_norm_g, B, T):
    nb = T // Q_BLOCK
    G = NSA_KV_HEADS

    def col(c0):
        return pl.BlockSpec((Q_BLOCK, HEAD_DIM), lambda b, g, i: (b * nb + i, c0 // HEAD_DIM + g))

    out_spec = pl.BlockSpec((None, None, None, Q_BLOCK, HEAD_DIM), lambda b, g, i: (b, g, i, 0, 0))
    shp = jax.ShapeDtypeStruct((B, G, nb, Q_BLOCK, HEAD_DIM), BF16)
    return pl.pallas_call(
        _kv_prep_kernel,
        grid=(B, G, nb),
        in_specs=[col(COL_KSL), col(COL_VSL), col(COL_KWN), col(COL_VWN),
                  pl.BlockSpec((3, HEAD_DIM), lambda b, g, i: (0, 0))],
        out_specs=[out_spec] * 4,
        out_shape=[shp] * 4,
        compiler_params=_cparams(("parallel", "parallel", "parallel")),
        name="kv_prep",
    )(proj, proj, proj, proj, k_norm_g)


def _compress_kernel(ck_ref, cv_ref, pos_ref, w1_ref, w2_ref, g_ref, kc_ref, vct_ref):
    nchunk = ck_ref.shape[0]
    half = CMP_STRIDE * HEAD_DIM

    def mlp(x_ref, idx):
        x = x_ref[...].astype(F32)
        xa = (x + pos_ref[idx, :, 0:half]).astype(BF16)
        xb = (x + pos_ref[idx, :, half:2 * half]).astype(BF16)
        u = jnp.dot(xa, w1_ref[idx, 0:half, :], preferred_element_type=F32)
        v = jnp.dot(xb, w1_ref[idx, half:2 * half, :], preferred_element_type=F32)
        pre = u + pltpu.roll(v, nchunk - 1, 0)
        return jnp.dot(jax.nn.gelu(pre).astype(BF16), w2_ref[idx], preferred_element_type=F32)

    kc = mlp(ck_ref, 0)
    ms = jnp.mean(kc * kc, axis=-1, keepdims=True)
    kc_ref[...] = (kc * lax.rsqrt(ms + EPS) * g_ref[0:1, :]).astype(BF16)
    vct_ref[...] = mlp(cv_ref, 1).T.astype(BF16)


def compress_kv(proj, cmp_pos, cmp_w1, cmp_w2, k_norm_g, B, T):
    G = NSA_KV_HEADS
    nchunk = T // CMP_STRIDE
    half = CMP_STRIDE * HEAD_DIM
    ckv = proj[:, COL_KC:COL_KC + 2 * D_KV].reshape(B, nchunk, CMP_STRIDE, 2, G, HEAD_DIM)
    ckv = ckv.transpose(3, 0, 4, 1, 2, 5).reshape(2, B, G, nchunk, half)
    pos = cmp_pos.reshape(2, 1, CMP_LEN * HEAD_DIM)
    in_spec = pl.BlockSpec((None, None, nchunk, half), lambda b, g: (b, g, 0, 0))
    return pl.pallas_call(
        _compress_kernel,
        grid=(B, G),
        in_specs=[in_spec, in_spec,
                  pl.BlockSpec((2, 1, CMP_LEN * HEAD_DIM), lambda b, g: (0, 0, 0)),
                  pl.BlockSpec((2, CMP_LEN * HEAD_DIM, CMP_HIDDEN), lambda b, g: (0, 0, 0)),
                  pl.BlockSpec((2, CMP_HIDDEN, HEAD_DIM), lambda b, g: (0, 0, 0)),
                  pl.BlockSpec((3, HEAD_DIM), lambda b, g: (0, 0))],
        out_specs=[pl.BlockSpec((None, None, nchunk, HEAD_DIM), lambda b, g: (b, g, 0, 0)),
                   pl.BlockSpec((None, None, HEAD_DIM, nchunk), lambda b, g: (b, g, 0, 0))],
        out_shape=[jax.ShapeDtypeStruct((B, G, nchunk, HEAD_DIM), BF16),
                   jax.ShapeDtypeStruct((B, G, HEAD_DIM, nchunk), BF16)],
        compiler_params=_cparams(("parallel", "parallel")),
        name="compress_kv",
    )(ckv[0], ckv[1], pos, cmp_w1.astype(BF16), cmp_w2.astype(BF16), k_norm_g)


def _nsa_kernel(q_ref, gate_ref, kc_ref, vct_ref, ks_ref, vst_ref, kw_ref, vwt_ref,
                tb_ref, cbt_ref, qg_ref, ovl_ref, o_ref, acc_ref, selb_ref, cb_ref, *, top_n):
    i = pl.program_id(2)
    nchunk = kc_ref.shape[0]
    n_s = ovl_ref.shape[0]
    scale = HEAD_DIM ** -0.5

    q = q_ref[...].astype(F32)
    qt = jnp.concatenate([q[:, h * HEAD_DIM:(h + 1) * HEAD_DIM].T for h in range(NSA_GROUP)], axis=1)
    ms = jnp.mean(qt * qt, axis=0, keepdims=True)
    q2t = (qt * lax.rsqrt(ms + EPS) * (qg_ref[...] * scale)).astype(BF16)

    crow = lax.broadcasted_iota(jnp.int32, (nchunk, GQ), 0)
    cb_ref[...] = jnp.where(crow < 8 * i - 8, cbt_ref[16:17, :], NEG_INF)

    @pl.when(i == 0)
    def _():
        cb_ref[0:8, :] = cbt_ref[8:16, :]

    @pl.when(i > 0)
    def _():
        cb_ref[pl.ds(pl.multiple_of(8 * i - 8, 8), 16), :] = cbt_ref[0:16, :]

    lc = jnp.dot(kc_ref[...], q2t, preferred_element_type=F32) + cb_ref[...]
    mc = jnp.max(lc, axis=0, keepdims=True)
    ec = jnp.where(lc > 0.5 * NEG_INF, jnp.exp(lc - mc), 0.0)
    pc = ec * (1.0 / jnp.maximum(jnp.sum(ec, axis=0, keepdims=True), 1e-30))
    o_cmp = jnp.dot(vct_ref[...], pc.astype(BF16), preferred_element_type=F32)

    pt = pc[:, 0:Q_BLOCK]
    for h in range(1, NSA_GROUP):
        pt = pt + pc[:, h * Q_BLOCK:(h + 1) * Q_BLOCK]
    ovl = ovl_ref[...]
    p1 = pt.astype(BF16)
    r1 = pt - p1.astype(F32)
    p2 = r1.astype(BF16)
    p3 = (r1 - p2.astype(F32)).astype(BF16)
    imp = (jnp.dot(ovl, p1, preferred_element_type=F32)
           + jnp.dot(ovl, p2, preferred_element_type=F32)
           + jnp.dot(ovl, p3, preferred_element_type=F32))
    sidx = lax.broadcasted_iota(jnp.int32, (n_s, Q_BLOCK), 0)
    rq = lax.broadcasted_iota(jnp.int32, (n_s, Q_BLOCK), 1)
    cur = (Q_BLOCK // SEL_LEN) * i + jnp.where(rq >= SEL_LEN, 1, 0)
    forced = (sidx == 0) | (sidx == cur) | (sidx == cur - 1)
    score = jnp.where(forced, FORCE_SCORE, jnp.where(sidx > cur, NEG_INF, imp))
    rank = jnp.zeros((n_s, Q_BLOCK), F32)
    for sp in range(n_s):
        row = score[sp:sp + 1, :]
        rank = rank + jnp.where(sidx > sp, jnp.where(row >= score, 1.0, 0.0), jnp.where(row > score, 1.0, 0.0))
    selb = jnp.where(rank < top_n, 0.0, NEG_INF)
    selb_ref[...] = jnp.concatenate([selb] * NSA_GROUP, axis=1)

    upper = lax.broadcasted_iota(jnp.int32, (Q_BLOCK, GQ), 0) < SEL_LEN

    def flash(k_ref, vt_ref, n_tiles, tile_of, kind_of, extra_bias):
        acc_ref[...] = jnp.zeros_like(acc_ref)

        def body(idx, carry):
            m, l = carry
            j = tile_of(idx)
            s = jnp.dot(k_ref[j], q2t, preferred_element_type=F32) + tb_ref[kind_of(idx)]
            if extra_bias is not None:
                s = s + extra_bias(j)
            m_new = jnp.maximum(m, jnp.max(s, axis=0, keepdims=True))
            alpha = jnp.exp(m - m_new)
            p = jnp.exp(s - m_new)
            l = alpha * l + jnp.sum(p, axis=0, keepdims=True)
            acc_ref[...] = acc_ref[...] * alpha + jnp.dot(vt_ref[j], p.astype(BF16), preferred_element_type=F32)
            return m_new, l

        m0 = jnp.full((1, GQ), NEG_INF, F32)
        l0 = jnp.zeros((1, GQ), F32)
        _, l = lax.fori_loop(0, n_tiles, body, (m0, l0))
        return acc_ref[...] * (1.0 / l)

    def sel_bias(j):
        b0 = selb_ref[pl.ds(2 * j, 1), :]
        b1 = selb_ref[pl.ds(2 * j + 1, 1), :]
        return jnp.where(upper, b0, b1)

    o_sel = flash(ks_ref, vst_ref, i + 1, lambda idx: idx,
                  lambda idx: jnp.minimum(i - idx, 2), sel_bias)

    o_win = flash(kw_ref, vwt_ref, jnp.minimum(i, WIN_TILES - 1) + 1, lambda idx: i - idx,
                  lambda idx: idx, None)

    gt = jax.nn.sigmoid(gate_ref[...].astype(F32)).T
    first_group = pl.program_id(1) == 0

    def gate_row(br):
        rows = []
        for h in range(NSA_GROUP):
            lo = 3 * h + br
            hi = 3 * (NSA_GROUP + h) + br
            rows.append(jnp.where(first_group, gt[lo:lo + 1, :], gt[hi:hi + 1, :]))
        return jnp.concatenate(rows, axis=1)

    ot = gate_row(0) * o_cmp + gate_row(1) * o_sel + gate_row(2) * o_win
    for h in range(NSA_GROUP):
        o_ref[:, h * HEAD_DIM:(h + 1) * HEAD_DIM] = ot[:, h * Q_BLOCK:(h + 1) * Q_BLOCK].T.astype(o_ref.dtype)


def _overlap_t(T):
    n_c = T // CMP_STRIDE
    n_s = T // SEL_LEN
    c_start = jnp.arange(n_c) * CMP_STRIDE
    c_end = c_start + CMP_LEN - 1
    s_start = jnp.arange(n_s) * SEL_LEN
    ov = (c_start[None, :] < s_start[:, None] + SEL_LEN) & (c_end[None, :] >= s_start[:, None])
    ov = ov & (jnp.arange(n_c)[None, :] < n_c - 1)
    return ov.astype(BF16)


def nsa_attention(proj, q_norm_g, kc, vct, ks, vst, kw, vwt, tb, cbt, B, T):
    nb = T // Q_BLOCK
    G = NSA_KV_HEADS
    nchunk = T // CMP_STRIDE
    n_s = T // SEL_LEN
    top_n = min(SEL_TOPN, n_s)
    kv_spec = pl.BlockSpec((None, None, nb, Q_BLOCK, HEAD_DIM), lambda b, g, i: (b, g, 0, 0, 0))
    gate_blk = COL_GATE // LANES
    return pl.pallas_call(
        functools.partial(_nsa_kernel, top_n=top_n),
        grid=(B, G, nb),
        in_specs=[pl.BlockSpec((Q_BLOCK, GQ), lambda b, g, i: (b * nb + i, g)),
                  pl.BlockSpec((Q_BLOCK, LANES), lambda b, g, i: (b * nb + i, gate_blk)),
                  pl.BlockSpec((None, None, nchunk, HEAD_DIM), lambda b, g, i: (b, g, 0, 0)),
                  pl.BlockSpec((None, None, HEAD_DIM, nchunk), lambda b, g, i: (b, g, 0, 0)),
                  kv_spec, kv_spec, kv_spec, kv_spec,
                  pl.BlockSpec((None, WIN_TILES, Q_BLOCK, GQ), lambda b, g, i: (g, 0, 0, 0)),
                  pl.BlockSpec((None, 24, GQ), lambda b, g, i: (g, 0, 0)),
                  pl.BlockSpec((HEAD_DIM, 1), lambda b, g, i: (0, 0)),
                  pl.BlockSpec((n_s, nchunk), lambda b, g, i: (0, 0))],
        out_specs=pl.BlockSpec((Q_BLOCK, GQ), lambda b, g, i: (b * nb + i, g)),
        out_shape=jax.ShapeDtypeStruct((B * T, D_NSA), BF16),
        scratch_shapes=[pltpu.VMEM((HEAD_DIM, GQ), F32),
                        pltpu.VMEM((n_s, GQ), F32),
                        pltpu.VMEM((nchunk, GQ), F32)],
        compiler_params=_cparams(("parallel", "parallel", "arbitrary")),
        name="nsa_attention",
    )(proj, proj, kc, vct, ks, vst, kw, vwt, tb, cbt, q_norm_g.reshape(HEAD_DIM, 1), _overlap_t(T))


def _ret_tables(T):
    H, C = RET_HEADS, RET_CHUNK
    half = RET_HEAD_DIM // 2
    pos = jnp.arange(T, dtype=F32)
    freqs = ROPE_BASE ** (-jnp.arange(half, dtype=F32) / half)
    ang = pos[:, None] * freqs[None, :]
    log_gamma = jnp.log(1.0 - 2.0 ** (-5.0 - jnp.arange(H, dtype=F32)))
    n = jnp.arange(C, dtype=F32)
    diff = n[:, None] - n[None, :]
    decay_in = jnp.where(diff >= 0, jnp.exp(jnp.maximum(diff, 0.0)[None] * log_gamma[:, None, None]), 0.0)
    q_decay = jnp.exp((n + 1.0)[None] * log_gamma[:, None])
    k_decay = jnp.exp((C - 1.0 - n)[None] * log_gamma[:, None])
    chunk_decay = jnp.exp(C * log_gamma)
    qk_decay = jnp.stack([q_decay, k_decay], axis=1)[..., None]
    return jnp.cos(ang), jnp.sin(ang), decay_in, qk_decay, chunk_decay


def _retention_kernel(cd_ref, q_ref, k_ref, v_ref, g_ref, cos_ref, sin_ref, din_ref, qkd_ref, gn_ref,
                      o_ref, state_ref):
    c = pl.program_id(1)
    d = RET_HEAD_DIM
    half = d // 2

    @pl.when(c == 0)
    def _():
        state_ref[...] = jnp.zeros_like(state_ref)

    cos = cos_ref[...]
    sin = sin_ref[...]

    def rope(x):
        x1, x2 = x[:, :half], x[:, half:]
        return jnp.concatenate([x1 * cos - x2 * sin, x2 * cos + x1 * sin], axis=1)

    for h in range(RET_HEADS):
        sl = slice(h * d, (h + 1) * d)
        qr = rope(q_ref[:, sl].astype(F32))
        kr = rope(k_ref[:, sl].astype(F32)) * (d ** -0.5)
        v = v_ref[:, sl]
        qb = qr.astype(BF16)
        inner = lax.dot_general(qb, kr.astype(BF16), (((1,), (1,)), ((), ())), preferred_element_type=F32)
        inner = (inner * din_ref[h]).astype(BF16)
        state = state_ref[h]
        cross = jnp.dot((qr * qkd_ref[h, 0]).astype(BF16), state.astype(BF16), preferred_element_type=F32)
        o = jnp.dot(inner, v, preferred_element_type=F32) + cross
        kdt = (kr * qkd_ref[h, 1]).T.astype(BF16)
        state_ref[h] = state * cd_ref[h] + jnp.dot(kdt, v, preferred_element_type=F32)
        mu = jnp.mean(o, axis=-1, keepdims=True)
        var = jnp.mean(jnp.square(o - mu), axis=-1, keepdims=True)
        on = (o - mu) * lax.rsqrt(var + EPS) * gn_ref[:, sl]
        o_ref[:, sl] = (jax.nn.silu(g_ref[:, sl].astype(F32)) * on).astype(o_ref.dtype)


def retention(proj, gn_g, B, T):
    nc = T // RET_CHUNK
    cos, sin, decay_in, qk_decay, chunk_decay = _ret_tables(T)
    cblk = COL_RET // D_RET

    def seg(k):
        return pl.BlockSpec((RET_CHUNK, D_RET), lambda b, c: (b * nc + c, cblk + k))

    rope_spec = pl.BlockSpec((RET_CHUNK, RET_HEAD_DIM // 2), lambda b, c: (c, 0))
    return pl.pallas_call(
        _retention_kernel,
        grid=(B, nc),
        in_specs=[pl.BlockSpec(memory_space=pltpu.SMEM),
                  seg(0), seg(1), seg(2), seg(3), rope_spec, rope_spec,
                  pl.BlockSpec((RET_HEADS, RET_CHUNK, RET_CHUNK), lambda b, c: (0, 0, 0)),
                  pl.BlockSpec((RET_HEADS, 2, RET_CHUNK, 1), lambda b, c: (0, 0, 0, 0)),
                  pl.BlockSpec((1, D_RET), lambda b, c: (0, 0))],
        out_specs=pl.BlockSpec((RET_CHUNK, D_RET), lambda b, c: (b * nc + c, 0)),
        out_shape=jax.ShapeDtypeStruct((B * T, D_RET), BF16),
        scratch_shapes=[pltpu.VMEM((RET_HEADS, RET_HEAD_DIM, RET_HEAD_DIM), F32)],
        compiler_params=_cparams(("parallel", "arbitrary")),
        name="retention",
    )(chunk_decay, proj, proj, proj, proj, cos, sin, decay_in, qk_decay, gn_g.reshape(1, D_RET))


def _pad_w_in(w):
    pad = jnp.zeros((w.shape[0], COL_RET - N_REAL_A), w.dtype)
    return jnp.concatenate([w[:, :N_REAL_A], pad, w[:, N_REAL_A:]], axis=1).astype(BF16)


def kernel(x, norm1_g, w_in, nsa_q_norm_g, nsa_k_norm_g, cmp_pos, cmp_w1, cmp_w2, ret_gn_g,
           w_up_nsa, w_up_ret, w_out, norm2_g, w_ff1, w_ff2, rel_bias):
    B, T, D = x.shape
    depth = w_in.shape[0]
    xf = x.reshape(B * T, D)
    tb, cbt = bias_tables(rel_bias)
    for l in range(depth):
        h = rmsnorm_bf16(xf, norm1_g[l])
        proj = matmul_bf16(h, _pad_w_in(w_in[l]), name="in_proj")
        ks, vst, kw, vwt = kv_prep(proj, nsa_k_norm_g[l], B, T)
        kc, vct = compress_kv(proj, cmp_pos[l], cmp_w1[l], cmp_w2[l], nsa_k_norm_g[l], B, T)
        o_a = nsa_attention(proj, nsa_q_norm_g[l], kc, vct, ks, vst, kw, vwt, tb, cbt, B, T)
        o_r = retention(proj, ret_gn_g[l], B, T)
        y = merge_up_proj(o_a, w_up_nsa[l].astype(BF16), o_r, w_up_ret[l].astype(BF16), proj)
        xf = matmul_residual(y, w_out[l].astype(BF16), xf, name="out_proj")
        h2 = rmsnorm_bf16(xf, norm2_g[l])
        a = matmul_bf16(h2, w_ff1[l].astype(BF16), relu2=True, name="ffn_up")
        xf = matmul_residual(a, w_ff2[l].astype(BF16), xf, name="ffn_down")
    return xf.reshape(B, T, D)
```

```python
import functools
import math

import jax
import jax.numpy as jnp
from jax import lax
from jax.experimental import pallas as pl
from jax.experimental.pallas import tpu as pltpu

F32 = jnp.float32
BF16 = jnp.bfloat16

D_MODEL = 4096
HEAD_DIM = 128
D_NSA = D_MODEL // 2
NSA_HEADS = D_NSA // HEAD_DIM
NSA_KV_HEADS = 2
NSA_GROUP = NSA_HEADS // NSA_KV_HEADS
D_KV = NSA_KV_HEADS * HEAD_DIM
CMP_LEN = 32
CMP_STRIDE = 16
CMP_HIDDEN = 256
SEL_LEN = 64
SEL_TOPN = 16
WINDOW = 512
Q_BLOCK = 128
D_RET = D_MODEL // 2
RET_HEADS = 8
RET_HEAD_DIM = D_RET // RET_HEADS
RET_CHUNK = 128
D_FF = 4 * D_MODEL
N_BUCKETS = 32
MAX_DISTANCE = 128
ROPE_BASE = 10000.0
EPS = 1e-6
NEG_INF = -1e30
FORCE_SCORE = 1e9

COL_Q = 0
COL_KC = D_NSA
COL_KSL = COL_KC + 2 * D_KV
COL_VSL = COL_KSL + D_KV
COL_KWN = COL_VSL + D_KV
COL_VWN = COL_KWN + D_KV
COL_GATE = COL_VWN + D_KV
N_GATE = 3 * NSA_HEADS
COL_RET = 4096
COL_MERGE_A = COL_RET + 4 * D_RET
COL_MERGE_R = COL_MERGE_A + D_MODEL
D_PROJ = COL_MERGE_R + D_MODEL
N_REAL_A = D_NSA + 6 * D_KV + N_GATE

LANES = 128
GQ = NSA_GROUP * Q_BLOCK
WIN_TILES = WINDOW // Q_BLOCK + 1
VMEM_LIMIT = 56 * 1024 * 1024


def _cparams(sem, vmem=VMEM_LIMIT):
    return pltpu.CompilerParams(dimension_semantics=sem, vmem_limit_bytes=vmem)


def _rmsnorm_kernel(x_ref, g_ref, o_ref):
    x = x_ref[...]
    ms = jnp.mean(x * x, axis=-1, keepdims=True)
    o_ref[...] = (x * lax.rsqrt(ms + EPS) * g_ref[...]).astype(o_ref.dtype)


def rmsnorm_bf16(x, g, tm=256):
    n, d = x.shape
    return pl.pallas_call(
        _rmsnorm_kernel,
        grid=(n // tm,),
        in_specs=[pl.BlockSpec((tm, d), lambda i: (i, 0)),
                  pl.BlockSpec((1, d), lambda i: (0, 0))],
        out_specs=pl.BlockSpec((tm, d), lambda i: (i, 0)),
        out_shape=jax.ShapeDtypeStruct((n, d), BF16),
        compiler_params=_cparams(("parallel",)),
        name="rmsnorm",
    )(x, g.reshape(1, d))


W_CAST_ROWS = 256


def _mm_ws_kernel(a_ref, w_ref, *rest, n_plain, shift, relu2):
    if shift:
        wt_ref, o_ref, wb_ref = rest
    else:
        o_ref, wb_ref = rest
    j = pl.program_id(0)
    k, tn = wb_ref.shape

    def cast_plain(r, _):
        rows = pl.ds(pl.multiple_of(r * W_CAST_ROWS, W_CAST_ROWS), W_CAST_ROWS)
        wb_ref[rows, :] = w_ref[rows, :].astype(BF16)
        return 0

    def cast_shifted(r, _):
        rows = pl.ds(pl.multiple_of(r * W_CAST_ROWS, W_CAST_ROWS), W_CAST_ROWS)
        wide = jnp.concatenate([w_ref[rows, :], wt_ref[rows, :]], axis=1)
        wb_ref[rows, :] = wide[:, shift:shift + tn].astype(BF16)
        return 0

    first = pl.program_id(1) == 0
    if shift:
        @pl.when(first & (j < n_plain))
        def _():
            lax.fori_loop(0, k // W_CAST_ROWS, cast_plain, 0)

        @pl.when(first & (j >= n_plain))
        def _():
            lax.fori_loop(0, k // W_CAST_ROWS, cast_shifted, 0)
    else:
        @pl.when(first)
        def _():
            lax.fori_loop(0, k // W_CAST_ROWS, cast_plain, 0)

    acc = jnp.dot(a_ref[...], wb_ref[...], preferred_element_type=F32)
    if relu2:
        acc = jnp.square(jnp.maximum(acc, 0.0))
    o_ref[...] = acc.astype(o_ref.dtype)


def matmul_wcast(a, w3, layer, *, n_out, n_plain=None, shift=0, relu2=False, tm=1024, tn=512, name="mm"):
    m, k = a.shape
    tm = min(tm, m)
    nj = n_out // tn
    n_plain = nj if n_plain is None else n_plain
    tail_w = LANES
    a_spec = pl.BlockSpec((tm, k), lambda j, i: (i, 0))
    if shift:
        in_specs = [a_spec,
                    pl.BlockSpec((None, k, tn), lambda j, i: (layer, 0, jnp.where(j < n_plain, j, j - 1))),
                    pl.BlockSpec((None, k, tail_w),
                                 lambda j, i: (layer, 0, jnp.where(j < n_plain, 0, j * (tn // tail_w))))]
        args = (a, w3, w3)
    else:
        in_specs = [a_spec, pl.BlockSpec((None, k, tn), lambda j, i: (layer, 0, j))]
        args = (a, w3)
    return pl.pallas_call(
        functools.partial(_mm_ws_kernel, n_plain=n_plain, shift=shift, relu2=relu2),
        grid=(nj, m // tm),
        in_specs=in_specs,
        out_specs=pl.BlockSpec((tm, tn), lambda j, i: (i, j)),
        out_shape=jax.ShapeDtypeStruct((m, n_out), BF16),
        scratch_shapes=[pltpu.VMEM((k, tn), BF16)],
        compiler_params=_cparams(("arbitrary", "arbitrary")),
        name=name,
    )(*args)


def _mm_res_kernel(a_ref, w_ref, r_ref, o_ref):
    k = pl.program_id(2)
    acc = jnp.dot(a_ref[...], w_ref[...], preferred_element_type=F32)

    @pl.when(k == 0)
    def _():
        o_ref[...] = r_ref[...] + acc

    @pl.when(k > 0)
    def _():
        o_ref[...] += acc


def matmul_residual(a, w3, layer, res, *, tm=1024, tn=1024, tk=2048, name="mm_res"):
    m, k = a.shape
    n = w3.shape[2]
    tm, tn, tk = min(tm, m), min(tn, n), min(tk, k)
    return pl.pallas_call(
        _mm_res_kernel,
        grid=(m // tm, n // tn, k // tk),
        in_specs=[pl.BlockSpec((tm, tk), lambda i, j, kk: (i, kk)),
                  pl.BlockSpec((None, tk, tn), lambda i, j, kk: (layer, kk, j)),
                  pl.BlockSpec((tm, tn), lambda i, j, kk: (i, j))],
        out_specs=pl.BlockSpec((tm, tn), lambda i, j, kk: (i, j)),
        out_shape=jax.ShapeDtypeStruct((m, n), F32),
        compiler_params=_cparams(("parallel", "arbitrary", "arbitrary")),
        name=name,
    )(a, w3, res)


def _mm_merge_kernel(a1_ref, w1_ref, a2_ref, w2_ref, g1_ref, g2_ref, o_ref):
    u1 = jnp.dot(a1_ref[...], w1_ref[...], preferred_element_type=F32)
    u2 = jnp.dot(a2_ref[...], w2_ref[...], preferred_element_type=F32)
    y = (jax.nn.sigmoid(g1_ref[...].astype(F32)) * u1
         + jax.nn.sigmoid(g2_ref[...].astype(F32)) * u2)
    o_ref[...] = y.astype(o_ref.dtype)


def merge_up_proj(o_a, w_a3, o_r, w_r3, layer, proj, *, tm=1024, tn=512):
    m, k = o_a.shape
    n = w_a3.shape[2]
    tm = min(tm, m)
    ca, cr = COL_MERGE_A // tn, COL_MERGE_R // tn
    return pl.pallas_call(
        _mm_merge_kernel,
        grid=(m // tm, n // tn),
        in_specs=[pl.BlockSpec((tm, k), lambda i, j: (i, 0)),
                  pl.BlockSpec((None, k, tn), lambda i, j: (layer, 0, j)),
                  pl.BlockSpec((tm, k), lambda i, j: (i, 0)),
                  pl.BlockSpec((None, k, tn), lambda i, j: (layer, 0, j)),
                  pl.BlockSpec((tm, tn), lambda i, j: (i, ca + j)),
                  pl.BlockSpec((tm, tn), lambda i, j: (i, cr + j))],
        out_specs=pl.BlockSpec((tm, tn), lambda i, j: (i, j)),
        out_shape=jax.ShapeDtypeStruct((m, n), BF16),
        compiler_params=_cparams(("parallel", "arbitrary")),
        name="merge_up",
    )(o_a, w_a3, o_r, w_r3, proj, proj)


def _t5_bucket(dist):
    n = jnp.maximum(dist, 0)
    max_exact = N_BUCKETS // 2
    large = max_exact + (jnp.log(jnp.maximum(n, 1).astype(F32) / max_exact)
                         / math.log(MAX_DISTANCE / max_exact) * (N_BUCKETS - max_exact)).astype(jnp.int32)
    large = jnp.minimum(large, N_BUCKETS - 1)
    return jnp.where(n < max_exact, n, large)


def _bucket_maps():
    kk = jnp.arange(Q_BLOCK)[:, None]
    r = jnp.arange(Q_BLOCK)[None, :]
    maps = []
    for kind in range(WIN_TILES):
        dist = kind * Q_BLOCK + r - kk
        valid = dist >= 0
        if kind == WIN_TILES - 1:
            valid = valid & (dist < WINDOW)
        maps.append(jnp.where(valid, _t5_bucket(dist), -1))
    tile_map = jnp.concatenate(maps, axis=0).astype(jnp.int32)
    mrow = jnp.arange(16)[:, None]
    dist_c = r - CMP_STRIDE * (mrow - 8) - (CMP_LEN - 1)
    cmp_rows = jnp.where(dist_c >= 0, _t5_bucket(dist_c), -1)
    far = jnp.full((8, Q_BLOCK), N_BUCKETS - 1)
    cmp_map = jnp.concatenate([cmp_rows, far], axis=0).astype(jnp.int32)
    return tile_map, cmp_map


def _bias_kernel(relb_ref, tmap_ref, cmap_ref, tb_ref, cb_ref):
    h = pl.program_id(0)

    def lookup(bmap):
        out = jnp.full(bmap.shape, NEG_INF, F32)
        for b in range(N_BUCKETS):
            out = jnp.where(bmap == b, relb_ref[b, h], out)
        return out

    tb_ref[...] = lookup(tmap_ref[...])
    cb_ref[...] = lookup(cmap_ref[...])


def bias_tables(rel_bias):
    tile_map, cmp_map = _bucket_maps()
    rt, rc = tile_map.shape[0], cmp_map.shape[0]
    tb, cb = pl.pallas_call(
        _bias_kernel,
        grid=(NSA_HEADS,),
        in_specs=[pl.BlockSpec(memory_space=pltpu.SMEM),
                  pl.BlockSpec((rt, LANES), lambda h: (0, 0)),
                  pl.BlockSpec((rc, LANES), lambda h: (0, 0))],
        out_specs=[pl.BlockSpec((None, rt, LANES), lambda h: (h // NSA_GROUP, 0, h % NSA_GROUP)),
                   pl.BlockSpec((None, rc, LANES), lambda h: (h // NSA_GROUP, 0, h % NSA_GROUP))],
        out_shape=[jax.ShapeDtypeStruct((NSA_KV_HEADS, rt, GQ), F32),
                   jax.ShapeDtypeStruct((NSA_KV_HEADS, rc, GQ), F32)],
        compiler_params=_cparams(("arbitrary",)),
        name="bias_tables",
    )(rel_bias, tile_map, cmp_map)
    return tb.reshape(NSA_KV_HEADS, WIN_TILES, Q_BLOCK, GQ), cb


def _kv_prep_kernel(ks_ref, vs_ref, kw_ref, vw_ref, g_ref, kso_ref, vso_ref, kwo_ref, vwo_ref):
    def norm(x, g):
        ms = jnp.mean(x * x, axis=-1, keepdims=True)
        return x * lax.rsqrt(ms + EPS) * g

    kso_ref[...] = norm(ks_ref[...].astype(F32), g_ref[1:2, :]).astype(BF16)
    kwo_ref[...] = norm(kw_ref[...].astype(F32), g_ref[2:3, :]).astype(BF16)
    vso_ref[...] = vs_ref[...].astype(F32).T.astype(BF16)
    vwo_ref[...] = vw_ref[...].astype(F32).T.astype(BF16)


def kv_prep(proj, k_norm_g, B, T):
    nb = T // Q_BLOCK
    G = NSA_KV_HEADS

    def col(c0):
        return pl.BlockSpec((Q_BLOCK, HEAD_DIM), lambda b, g, i: (b * nb + i, c0 // HEAD_DIM + g))

    out_spec = pl.BlockSpec((None, None, None, Q_BLOCK, HEAD_DIM), lambda b, g, i: (b, g, i, 0, 0))
    shp = jax.ShapeDtypeStruct((B, G, nb, Q_BLOCK, HEAD_DIM), BF16)
    return pl.pallas_call(
        _kv_prep_kernel,
        grid=(B, G, nb),
        in_specs=[col(COL_KSL), col(COL_VSL), col(COL_KWN), col(COL_VWN),
                  pl.BlockSpec((3, HEAD_DIM), lambda b, g, i: (0, 0))],
        out_specs=[out_spec] * 4,
        out_shape=[shp] * 4,
        compiler_params=_cparams(("parallel", "parallel", "parallel")),
        name="kv_prep",
    )(proj, proj, proj, proj, k_norm_g)


def _compress_kernel(ck_ref, cv_ref, pos_ref, w1_ref, w2_ref, g_ref, kc_ref, vct_ref):
    nchunk = ck_ref.shape[0]
    half = CMP_STRIDE * HEAD_DIM

    def mlp(x_ref, idx):
        x = x_ref[...].astype(F32)
        xa = (x + pos_ref[idx, :, 0:half]).astype(BF16)
        xb = (x + pos_ref[idx, :, half:2 * half]).astype(BF16)
        u = jnp.dot(xa, w1_ref[idx, 0:half, :], preferred_element_type=F32)
        v = jnp.dot(xb, w1_ref[idx, half:2 * half, :], preferred_element_type=F32)
        pre = u + pltpu.roll(v, nchunk - 1, 0)
        return jnp.dot(jax.nn.gelu(pre).astype(BF16), w2_ref[idx], preferred_element_type=F32)

    kc = mlp(ck_ref, 0)
    ms = jnp.mean(kc * kc, axis=-1, keepdims=True)
    kc_ref[...] = (kc * lax.rsqrt(ms + EPS) * g_ref[0:1, :]).astype(BF16)
    vct_ref[...] = mlp(cv_ref, 1).T.astype(BF16)


def compress_kv(proj, cmp_pos, cmp_w1, cmp_w2, k_norm_g, B, T):
    G = NSA_KV_HEADS
    nchunk = T // CMP_STRIDE
    half = CMP_STRIDE * HEAD_DIM
    ckv = proj[:, COL_KC:COL_KC + 2 * D_KV].reshape(B, nchunk, CMP_STRIDE, 2, G, HEAD_DIM)
    ckv = ckv.transpose(3, 0, 4, 1, 2, 5).reshape(2, B, G, nchunk, half)
    pos = cmp_pos.reshape(2, 1, CMP_LEN * HEAD_DIM)
    in_spec = pl.BlockSpec((None, None, nchunk, half), lambda b, g: (b, g, 0, 0))
    return pl.pallas_call(
        _compress_kernel,
        grid=(B, G),
        in_specs=[in_spec, in_spec,
                  pl.BlockSpec((2, 1, CMP_LEN * HEAD_DIM), lambda b, g: (0, 0, 0)),
                  pl.BlockSpec((2, CMP_LEN * HEAD_DIM, CMP_HIDDEN), lambda b, g: (0, 0, 0)),
                  pl.BlockSpec((2, CMP_HIDDEN, HEAD_DIM), lambda b, g: (0, 0, 0)),
                  pl.BlockSpec((3, HEAD_DIM), lambda b, g: (0, 0))],
        out_specs=[pl.BlockSpec((None, None, nchunk, HEAD_DIM), lambda b, g: (b, g, 0, 0)),
                   pl.BlockSpec((None, None, HEAD_DIM, nchunk), lambda b, g: (b, g, 0, 0))],
        out_shape=[jax.ShapeDtypeStruct((B, G, nchunk, HEAD_DIM), BF16),
                   jax.ShapeDtypeStruct((B, G, HEAD_DIM, nchunk), BF16)],
        compiler_params=_cparams(("parallel", "parallel")),
        name="compress_kv",
    )(ckv[0], ckv[1], pos, cmp_w1.astype(BF16), cmp_w2.astype(BF16), k_norm_g)


def _nsa_kernel(q_ref, gate_ref, kc_ref, vct_ref, ks_ref, vst_ref, kw_ref, vwt_ref,
                tb_ref, cbt_ref, qg_ref, ovl_ref, o_ref, acc_ref, selb_ref, cb_ref, *, top_n):
    i = pl.program_id(2)
    nchunk = kc_ref.shape[0]
    n_s = ovl_ref.shape[0]
    scale = HEAD_DIM ** -0.5

    q = q_ref[...].astype(F32)
    qt = jnp.concatenate([q[:, h * HEAD_DIM:(h + 1) * HEAD_DIM].T for h in range(NSA_GROUP)], axis=1)
    ms = jnp.mean(qt * qt, axis=0, keepdims=True)
    q2t = (qt * lax.rsqrt(ms + EPS) * (qg_ref[...] * scale)).astype(BF16)

    crow = lax.broadcasted_iota(jnp.int32, (nchunk, GQ), 0)
    cb_ref[...] = jnp.where(crow < 8 * i - 8, cbt_ref[16:17, :], NEG_INF)

    @pl.when(i == 0)
    def _():
        cb_ref[0:8, :] = cbt_ref[8:16, :]

    @pl.when(i > 0)
    def _():
        cb_ref[pl.ds(pl.multiple_of(8 * i - 8, 8), 16), :] = cbt_ref[0:16, :]

    lc = jnp.dot(kc_ref[...], q2t, preferred_element_type=F32) + cb_ref[...]
    mc = jnp.max(lc, axis=0, keepdims=True)
    ec = jnp.where(lc > 0.5 * NEG_INF, jnp.exp(lc - mc), 0.0)
    pc = ec * (1.0 / jnp.maximum(jnp.sum(ec, axis=0, keepdims=True), 1e-30))
    o_cmp = jnp.dot(vct_ref[...], pc.astype(BF16), preferred_element_type=F32)

    pt = pc[:, 0:Q_BLOCK]
    for h in range(1, NSA_GROUP):
        pt = pt + pc[:, h * Q_BLOCK:(h + 1) * Q_BLOCK]
    ovl = ovl_ref[...]
    p1 = pt.astype(BF16)
    r1 = pt - p1.astype(F32)
    p2 = r1.astype(BF16)
    p3 = (r1 - p2.astype(F32)).astype(BF16)
    imp = (jnp.dot(ovl, p1, preferred_element_type=F32)
           + jnp.dot(ovl, p2, preferred_element_type=F32)
           + jnp.dot(ovl, p3, preferred_element_type=F32))
    sidx = lax.broadcasted_iota(jnp.int32, (n_s, Q_BLOCK), 0)
    rq = lax.broadcasted_iota(jnp.int32, (n_s, Q_BLOCK), 1)
    cur = (Q_BLOCK // SEL_LEN) * i + jnp.where(rq >= SEL_LEN, 1, 0)
    forced = (sidx == 0) | (sidx == cur) | (sidx == cur - 1)
    score = jnp.where(forced, FORCE_SCORE, jnp.where(sidx > cur, NEG_INF, imp))
    rank = jnp.zeros((n_s, Q_BLOCK), F32)
    for sp in range(n_s):
        row = score[sp:sp + 1, :]
        rank = rank + jnp.where(sidx > sp, jnp.where(row >= score, 1.0, 0.0), jnp.where(row > score, 1.0, 0.0))
    selb = jnp.where(rank < top_n, 0.0, NEG_INF)
    selb_ref[...] = jnp.concatenate([selb] * NSA_GROUP, axis=1)

    upper = lax.broadcasted_iota(jnp.int32, (Q_BLOCK, GQ), 0) < SEL_LEN

    def flash(k_ref, vt_ref, n_tiles, tile_of, kind_of, extra_bias):
        acc_ref[...] = jnp.zeros_like(acc_ref)

        def body(idx, carry):
            m, l = carry
            j = tile_of(idx)
            s = jnp.dot(k_ref[j], q2t, preferred_element_type=F32) + tb_ref[kind_of(idx)]
            if extra_bias is not None:
                s = s + extra_bias(j)
            m_new = jnp.maximum(m, jnp.max(s, axis=0, keepdims=True))
            alpha = jnp.exp(m - m_new)
            p = jnp.exp(s - m_new)
            l = alpha * l + jnp.sum(p, axis=0, keepdims=True)
            acc_ref[...] = acc_ref[...] * alpha + jnp.dot(vt_ref[j], p.astype(BF16), preferred_element_type=F32)
            return m_new, l

        m0 = jnp.full((1, GQ), NEG_INF, F32)
        l0 = jnp.zeros((1, GQ), F32)
        _, l = lax.fori_loop(0, n_tiles, body, (m0, l0))
        return acc_ref[...] * (1.0 / l)

    def sel_bias(j):
        b0 = selb_ref[pl.ds(2 * j, 1), :]
        b1 = selb_ref[pl.ds(2 * j + 1, 1), :]
        return jnp.where(upper, b0, b1)

    o_sel = flash(ks_ref, vst_ref, i + 1, lambda idx: idx,
                  lambda idx: jnp.minimum(i - idx, 2), sel_bias)

    o_win = flash(kw_ref, vwt_ref, jnp.minimum(i, WIN_TILES - 1) + 1, lambda idx: i - idx,
                  lambda idx: idx, None)

    gt = jax.nn.sigmoid(gate_ref[...].astype(F32)).T
    first_group = pl.program_id(1) == 0

    def gate_row(br):
        rows = []
        for h in range(NSA_GROUP):
            lo = 3 * h + br
            hi = 3 * (NSA_GROUP + h) + br
            rows.append(jnp.where(first_group, gt[lo:lo + 1, :], gt[hi:hi + 1, :]))
        return jnp.concatenate(rows, axis=1)

    ot = gate_row(0) * o_cmp + gate_row(1) * o_sel + gate_row(2) * o_win
    for h in range(NSA_GROUP):
        o_ref[:, h * HEAD_DIM:(h + 1) * HEAD_DIM] = ot[:, h * Q_BLOCK:(h + 1) * Q_BLOCK].T.astype(o_ref.dtype)


def _overlap_t(T):
    n_c = T // CMP_STRIDE
    n_s = T // SEL_LEN
    c_start = jnp.arange(n_c) * CMP_STRIDE
    c_end = c_start + CMP_LEN - 1
    s_start = jnp.arange(n_s) * SEL_LEN
    ov = (c_start[None, :] < s_start[:, None] + SEL_LEN) & (c_end[None, :] >= s_start[:, None])
    ov = ov & (jnp.arange(n_c)[None, :] < n_c - 1)
    return ov.astype(BF16)


def nsa_attention(proj, q_norm_g, kc, vct, ks, vst, kw, vwt, tb, cbt, B, T):
    nb = T // Q_BLOCK
    G = NSA_KV_HEADS
    nchunk = T // CMP_STRIDE
    n_s = T // SEL_LEN
    top_n = min(SEL_TOPN, n_s)
    kv_spec = pl.BlockSpec((None, None, nb, Q_BLOCK, HEAD_DIM), lambda b, g, i: (b, g, 0, 0, 0))
    gate_blk = COL_GATE // LANES
    return pl.pallas_call(
        functools.partial(_nsa_kernel, top_n=top_n),
        grid=(B, G, nb),
        in_specs=[pl.BlockSpec((Q_BLOCK, GQ), lambda b, g, i: (b * nb + i, g)),
                  pl.BlockSpec((Q_BLOCK, LANES), lambda b, g, i: (b * nb + i, gate_blk)),
                  pl.BlockSpec((None, None, nchunk, HEAD_DIM), lambda b, g, i: (b, g, 0, 0)),
                  pl.BlockSpec((None, None, HEAD_DIM, nchunk), lambda b, g, i: (b, g, 0, 0)),
                  kv_spec, kv_spec, kv_spec, kv_spec,
                  pl.BlockSpec((None, WIN_TILES, Q_BLOCK, GQ), lambda b, g, i: (g, 0, 0, 0)),
                  pl.BlockSpec((None, 24, GQ), lambda b, g, i: (g, 0, 0)),
                  pl.BlockSpec((HEAD_DIM, 1), lambda b, g, i: (0, 0)),
                  pl.BlockSpec((n_s, nchunk), lambda b, g, i: (0, 0))],
        out_specs=pl.BlockSpec((Q_BLOCK, GQ), lambda b, g, i: (b * nb + i, g)),
        out_shape=jax.ShapeDtypeStruct((B * T, D_NSA), BF16),
        scratch_shapes=[pltpu.VMEM((HEAD_DIM, GQ), F32),
                        pltpu.VMEM((n_s, GQ), F32),
                        pltpu.VMEM((nchunk, GQ), F32)],
        compiler_params=_cparams(("parallel", "parallel", "arbitrary")),
        name="nsa_attention",
    )(proj, proj, kc, vct, ks, vst, kw, vwt, tb, cbt, q_norm_g.reshape(HEAD_DIM, 1), _overlap_t(T))


def _ret_tables(T):
    H, C = RET_HEADS, RET_CHUNK
    half = RET_HEAD_DIM // 2
    pos = jnp.arange(T, dtype=F32)
    freqs = ROPE_BASE ** (-jnp.arange(half, dtype=F32) / half)
    ang = pos[:, None] * freqs[None, :]
    log_gamma = jnp.log(1.0 - 2.0 ** (-5.0 - jnp.arange(H, dtype=F32)))
    n = jnp.arange(C, dtype=F32)
    diff = n[:, None] - n[None, :]
    decay_in = jnp.where(diff >= 0, jnp.exp(jnp.maximum(diff, 0.0)[None] * log_gamma[:, None, None]), 0.0)
    q_decay = jnp.exp((n + 1.0)[None] * log_gamma[:, None])
    k_decay = jnp.exp((C - 1.0 - n)[None] * log_gamma[:, None])
    chunk_decay = jnp.exp(C * log_gamma)
    qk_decay = jnp.stack([q_decay, k_decay], axis=1)[..., None]
    return jnp.cos(ang), jnp.sin(ang), decay_in, qk_decay, chunk_decay


def _retention_kernel(cd_ref, q_ref, k_ref, v_ref, g_ref, cos_ref, sin_ref, din_ref, qkd_ref, gn_ref,
                      o_ref, state_ref):
    c = pl.program_id(1)
    d = RET_HEAD_DIM
    half = d // 2

    @pl.when(c == 0)
    def _():
        state_ref[...] = jnp.zeros_like(state_ref)

    cos = cos_ref[...]
    sin = sin_ref[...]

    def rope(x):
        x1, x2 = x[:, :half], x[:, half:]
        return jnp.concatenate([x1 * cos - x2 * sin, x2 * cos + x1 * sin], axis=1)

    for h in range(RET_HEADS):
        sl = slice(h * d, (h + 1) * d)
        qr = rope(q_ref[:, sl].astype(F32))
        kr = rope(k_ref[:, sl].astype(F32)) * (d ** -0.5)
        v = v_ref[:, sl]
        qb = qr.astype(BF16)
        inner = lax.dot_general(qb, kr.astype(BF16), (((1,), (1,)), ((), ())), preferred_element_type=F32)
        inner = (inner * din_ref[h]).astype(BF16)
        state = state_ref[h]
        cross = jnp.dot((qr * qkd_ref[h, 0]).astype(BF16), state.astype(BF16), preferred_element_type=F32)
        o = jnp.dot(inner, v, preferred_element_type=F32) + cross
        kdt = (kr * qkd_ref[h, 1]).T.astype(BF16)
        state_ref[h] = state * cd_ref[h] + jnp.dot(kdt, v, preferred_element_type=F32)
        mu = jnp.mean(o, axis=-1, keepdims=True)
        var = jnp.mean(jnp.square(o - mu), axis=-1, keepdims=True)
        on = (o - mu) * lax.rsqrt(var + EPS) * gn_ref[:, sl]
        o_ref[:, sl] = (jax.nn.silu(g_ref[:, sl].astype(F32)) * on).astype(o_ref.dtype)


def retention(proj, gn_g, B, T):
    nc = T // RET_CHUNK
    cos, sin, decay_in, qk_decay, chunk_decay = _ret_tables(T)
    cblk = COL_RET // D_RET

    def seg(k):
        return pl.BlockSpec((RET_CHUNK, D_RET), lambda b, c: (b * nc + c, cblk + k))

    rope_spec = pl.BlockSpec((RET_CHUNK, RET_HEAD_DIM // 2), lambda b, c: (c, 0))
    return pl.pallas_call(
        _retention_kernel,
        grid=(B, nc),
        in_specs=[pl.BlockSpec(memory_space=pltpu.SMEM),
                  seg(0), seg(1), seg(2), seg(3), rope_spec, rope_spec,
                  pl.BlockSpec((RET_HEADS, RET_CHUNK, RET_CHUNK), lambda b, c: (0, 0, 0)),
                  pl.BlockSpec((RET_HEADS, 2, RET_CHUNK, 1), lambda b, c: (0, 0, 0, 0)),
                  pl.BlockSpec((1, D_RET), lambda b, c: (0, 0))],
        out_specs=pl.BlockSpec((RET_CHUNK, D_RET), lambda b, c: (b * nc + c, 0)),
        out_shape=jax.ShapeDtypeStruct((B * T, D_RET), BF16),
        scratch_shapes=[pltpu.VMEM((RET_HEADS, RET_HEAD_DIM, RET_HEAD_DIM), F32)],
        compiler_params=_cparams(("parallel", "arbitrary")),
        name="retention",
    )(chunk_decay, proj, proj, proj, proj, cos, sin, decay_in, qk_decay, gn_g.reshape(1, D_RET))


def kernel(x, norm1_g, w_in, nsa_q_norm_g, nsa_k_norm_g, cmp_pos, cmp_w1, cmp_w2, ret_gn_g,
           w_up_nsa, w_up_ret, w_out, norm2_g, w_ff1, w_ff2, rel_bias):
    B, T, D = x.shape
    depth = w_in.shape[0]
    xf = x.reshape(B * T, D)
    tb, cbt = bias_tables(rel_bias)
    w_up_nsa_b, w_up_ret_b = w_up_nsa.astype(BF16), w_up_ret.astype(BF16)
    w_out_b, w_ff2_b = w_out.astype(BF16), w_ff2.astype(BF16)
    tn_in = 512
    assert (COL_RET // tn_in - 1) * tn_in + N_REAL_A % tn_in == N_REAL_A
    for l in range(depth):
        h = rmsnorm_bf16(xf, norm1_g[l])
        proj = matmul_wcast(h, w_in, l, n_out=D_PROJ, n_plain=COL_RET // tn_in, shift=N_REAL_A % tn_in,
                            tn=tn_in, name="in_proj")
        ks, vst, kw, vwt = kv_prep(proj, nsa_k_norm_g[l], B, T)
        kc, vct = compress_kv(proj, cmp_pos[l], cmp_w1[l], cmp_w2[l], nsa_k_norm_g[l], B, T)
        o_a = nsa_attention(proj, nsa_q_norm_g[l], kc, vct, ks, vst, kw, vwt, tb, cbt, B, T)
        o_r = retention(proj, ret_gn_g[l], B, T)
        y = merge_up_proj(o_a, w_up_nsa_b, o_r, w_up_ret_b, l, proj)
        xf = matmul_residual(y, w_out_b, l, xf, name="out_proj")
        h2 = rmsnorm_bf16(xf, norm2_g[l])
        a = matmul_wcast(h2, w_ff1, l, n_out=D_FF, relu2=True, name="ffn_up")
        xf = matmul_residual(a, w_ff2_b, l, xf, name="ffn_down")
    return xf.reshape(B, T, D)
```

```python
import functools
import math

import jax
import jax.numpy as jnp
from jax import lax
from jax.experimental import pallas as pl
from jax.experimental.pallas import tpu as pltpu

F32 = jnp.float32
BF16 = jnp.bfloat16

D_MODEL = 4096
HEAD_DIM = 128
D_NSA = D_MODEL // 2
NSA_HEADS = D_NSA // HEAD_DIM
NSA_KV_HEADS = 2
NSA_GROUP = NSA_HEADS // NSA_KV_HEADS
D_KV = NSA_KV_HEADS * HEAD_DIM
CMP_LEN = 32
CMP_STRIDE = 16
CMP_HIDDEN = 256
SEL_LEN = 64
SEL_TOPN = 16
WINDOW = 512
Q_BLOCK = 128
D_RET = D_MODEL // 2
RET_HEADS = 8
RET_HEAD_DIM = D_RET // RET_HEADS
RET_CHUNK = 128
D_FF = 4 * D_MODEL
N_BUCKETS = 32
MAX_DISTANCE = 128
ROPE_BASE = 10000.0
EPS = 1e-6
NEG_INF = -1e30
FORCE_SCORE = 1e9
LOG2E = 1.4426950408889634

COL_Q = 0
COL_KC = D_NSA
COL_KSL = COL_KC + 2 * D_KV
COL_VSL = COL_KSL + D_KV
COL_KWN = COL_VSL + D_KV
COL_VWN = COL_KWN + D_KV
COL_GATE = COL_VWN + D_KV
N_GATE = 3 * NSA_HEADS
COL_RET = 4096
COL_MERGE_A = COL_RET + 4 * D_RET
COL_MERGE_R = COL_MERGE_A + D_MODEL
D_PROJ = COL_MERGE_R + D_MODEL
N_REAL_A = D_NSA + 6 * D_KV + N_GATE

LANES = 128
GQ = NSA_GROUP * Q_BLOCK
KT = 2 * Q_BLOCK
K_AUG = HEAD_DIM + LANES
BIAS_ROW0 = HEAD_DIM
SEL_ROW0 = HEAD_DIM + 16
CMP_PER_QB = Q_BLOCK // CMP_STRIDE
CMP_NEAR = 2 * CMP_PER_QB
VMEM_LIMIT = 56 * 1024 * 1024


def _cparams(sem, vmem=VMEM_LIMIT):
    return pltpu.CompilerParams(dimension_semantics=sem, vmem_limit_bytes=vmem)


def _rmsnorm_kernel(x_ref, g_ref, o_ref):
    x = x_ref[...]
    ms = jnp.mean(x * x, axis=-1, keepdims=True)
    o_ref[...] = (x * lax.rsqrt(ms + EPS) * g_ref[...]).astype(o_ref.dtype)


def rmsnorm_bf16(x, g, tm=256):
    n, d = x.shape
    return pl.pallas_call(
        _rmsnorm_kernel,
        grid=(n // tm,),
        in_specs=[pl.BlockSpec((tm, d), lambda i: (i, 0)),
                  pl.BlockSpec((1, d), lambda i: (0, 0))],
        out_specs=pl.BlockSpec((tm, d), lambda i: (i, 0)),
        out_shape=jax.ShapeDtypeStruct((n, d), BF16),
        compiler_params=_cparams(("parallel",)),
        name="rmsnorm",
    )(x, g.reshape(1, d))


W_CAST_VREGS = 256


def _mm_ws_kernel(a_ref, w_hbm, o_ref, stage_ref, wb_ref, sem, *, layer, transposed, n_plain, seg_start, relu2):
    j = pl.program_id(0)
    i = pl.program_id(1)
    nj = pl.num_programs(0)
    tn = o_ref.shape[1]

    def w_copy(jj):
        if transposed:
            start = jnp.where(jj < n_plain, jj * tn, seg_start + (jj - n_plain) * tn)
            src = w_hbm.at[layer, pl.ds(pl.multiple_of(start, 16), tn), :]
        else:
            src = w_hbm.at[layer, :, pl.ds(pl.multiple_of(jj * tn, tn), tn)]
        return pltpu.make_async_copy(src, stage_ref, sem)

    @pl.when(i == 0)
    def _():
        @pl.when(j == 0)
        def _():
            w_copy(j).start()

        w_copy(j).wait()
        rows, cols = stage_ref.shape
        step = min(rows, W_CAST_VREGS * 8 * LANES // cols)
        assert rows % step == 0

        def cast(r, _):
            sl = pl.ds(pl.multiple_of(r * step, step), step)
            wb_ref[sl, :] = stage_ref[sl, :].astype(BF16)
            return 0

        lax.fori_loop(0, rows // step, cast, 0)

        @pl.when(j + 1 < nj)
        def _():
            w_copy(j + 1).start()

    if transposed:
        acc = lax.dot_general(a_ref[...], wb_ref[...], (((1,), (1,)), ((), ())), preferred_element_type=F32)
    else:
        acc = jnp.dot(a_ref[...], wb_ref[...], preferred_element_type=F32)
    if relu2:
        acc = jnp.square(jnp.maximum(acc, 0.0))
    o_ref[...] = acc.astype(o_ref.dtype)


def matmul_wcast(a, w3, layer, *, n_out, transposed=False, n_plain=None, seg_start=0, relu2=False,
                 tm=1024, tn=1024, name="mm"):
    m, k = a.shape
    tm = min(tm, m)
    nj = n_out // tn
    n_plain = nj if n_plain is None else n_plain
    assert transposed or n_plain == nj
    stage_shape = (tn, k) if transposed else (k, tn)
    return pl.pallas_call(
        functools.partial(_mm_ws_kernel, layer=layer, transposed=transposed, n_plain=n_plain,
                          seg_start=seg_start, relu2=relu2),
        grid=(nj, m // tm),
        in_specs=[pl.BlockSpec((tm, k), lambda j, i: (i, 0)),
                  pl.BlockSpec(memory_space=pl.ANY)],
        out_specs=pl.BlockSpec((tm, tn), lambda j, i: (i, j)),
        out_shape=jax.ShapeDtypeStruct((m, n_out), BF16),
        scratch_shapes=[pltpu.VMEM(stage_shape, F32),
                        pltpu.VMEM(stage_shape, BF16),
                        pltpu.SemaphoreType.DMA(())],
        compiler_params=_cparams(("arbitrary", "arbitrary")),
        name=name,
    )(a, w3)


def _mm_res_kernel(a_ref, w_ref, r_ref, o_ref):
    k = pl.program_id(2)
    acc = jnp.dot(a_ref[...], w_ref[...], preferred_element_type=F32)

    @pl.when(k == 0)
    def _():
        o_ref[...] = r_ref[...] + acc

    @pl.when(k > 0)
    def _():
        o_ref[...] += acc


def matmul_residual(a, w3, layer, res, *, tm=1024, tn=1024, tk=2048, name="mm_res"):
    m, k = a.shape
    n = w3.shape[2]
    tm, tn, tk = min(tm, m), min(tn, n), min(tk, k)
    return pl.pallas_call(
        _mm_res_kernel,
        grid=(m // tm, n // tn, k // tk),
        in_specs=[pl.BlockSpec((tm, tk), lambda i, j, kk: (i, kk)),
                  pl.BlockSpec((None, tk, tn), lambda i, j, kk: (layer, kk, j)),
                  pl.BlockSpec((tm, tn), lambda i, j, kk: (i, j))],
        out_specs=pl.BlockSpec((tm, tn), lambda i, j, kk: (i, j)),
        out_shape=jax.ShapeDtypeStruct((m, n), F32),
        compiler_params=_cparams(("parallel", "arbitrary", "arbitrary")),
        name=name,
    )(a, w3, res)


def _mm_merge_kernel(a1_ref, w1_ref, a2_ref, w2_ref, g1_ref, g2_ref, o_ref):
    u1 = jnp.dot(a1_ref[...], w1_ref[...], preferred_element_type=F32)
    u2 = jnp.dot(a2_ref[...], w2_ref[...], preferred_element_type=F32)
    y = (jax.nn.sigmoid(g1_ref[...].astype(F32)) * u1
         + jax.nn.sigmoid(g2_ref[...].astype(F32)) * u2)
    o_ref[...] = y.astype(o_ref.dtype)


def merge_up_proj(o_a, w_a3, o_r, w_r3, layer, proj, *, tm=1024, tn=512):
    m, k = o_a.shape
    n = w_a3.shape[2]
    tm = min(tm, m)
    ca, cr = COL_MERGE_A // tn, COL_MERGE_R // tn
    return pl.pallas_call(
        _mm_merge_kernel,
        grid=(m // tm, n // tn),
        in_specs=[pl.BlockSpec((tm, k), lambda i, j: (i, 0)),
                  pl.BlockSpec((None, k, tn), lambda i, j: (layer, 0, j)),
                  pl.BlockSpec((tm, k), lambda i, j: (i, 0)),
                  pl.BlockSpec((None, k, tn), lambda i, j: (layer, 0, j)),
                  pl.BlockSpec((tm, tn), lambda i, j: (i, ca + j)),
                  pl.BlockSpec((tm, tn), lambda i, j: (i, cr + j))],
        out_specs=pl.BlockSpec((tm, tn), lambda i, j: (i, j)),
        out_shape=jax.ShapeDtypeStruct((m, n), BF16),
        compiler_params=_cparams(("parallel", "arbitrary")),
        name="merge_up",
    )(o_a, w_a3, o_r, w_r3, proj, proj)


def _t5_bucket(dist):
    n = jnp.maximum(dist, 0)
    max_exact = N_BUCKETS // 2
    large = max_exact + (jnp.log(jnp.maximum(n, 1).astype(F32) / max_exact)
                         / math.log(MAX_DISTANCE / max_exact) * (N_BUCKETS - max_exact)).astype(jnp.int32)
    large = jnp.minimum(large, N_BUCKETS - 1)
    return jnp.where(n < max_exact, n, large)


N_PAIR_TABLES = 6


def _bucket_maps():
    kk = jnp.arange(Q_BLOCK)[:, None]
    r = jnp.arange(Q_BLOCK)[None, :]

    def tile(delta):
        dist = delta * Q_BLOCK + r - kk
        return jnp.where((dist >= 0) & (dist < WINDOW), _t5_bucket(dist), -1)

    neg = jnp.full((Q_BLOCK, Q_BLOCK), -1)
    far = jnp.full((Q_BLOCK, Q_BLOCK), N_BUCKETS - 1)
    pairs = [(tile(0), neg), (far, tile(1)), (tile(4), far),
             (tile(1), tile(0)), (far, far), (neg, tile(4))]
    pair_map = jnp.concatenate([jnp.concatenate(p, axis=0) for p in pairs], axis=0).astype(jnp.int32)
    mrow = jnp.arange(CMP_NEAR)[:, None]
    dist_c = r - CMP_STRIDE * (mrow - CMP_PER_QB) - (CMP_LEN - 1)
    cmp_rows = jnp.where(dist_c >= 0, _t5_bucket(dist_c), -1)
    cmp_map = jnp.concatenate([cmp_rows, jnp.full((8, Q_BLOCK), N_BUCKETS - 1)], axis=0).astype(jnp.int32)
    return pair_map, cmp_map


def _bias_kernel(relb_ref, pmap_ref, cmap_ref, tp_ref, cb_ref, br_ref):
    h = pl.program_id(0)
    far = relb_ref[N_BUCKETS - 1, h]

    def lookup(bmap, offset):
        out = jnp.full(bmap.shape, NEG_INF, F32)
        for b in range(N_BUCKETS):
            out = jnp.where(bmap == b, (relb_ref[b, h] - offset) * LOG2E, out)
        return out

    tp_ref[...] = lookup(pmap_ref[...], far)
    cb_ref[...] = lookup(cmap_ref[...], 0.0)
    full = jnp.full(br_ref.shape, far * LOG2E, F32)
    hi = full.astype(BF16).astype(F32)
    row = lax.broadcasted_iota(jnp.int32, br_ref.shape, 0)
    br_ref[...] = jnp.where(row == 0, hi, jnp.where(row == 1, full - hi, 0.0))


def bias_tables(rel_bias):
    pair_map, cmp_map = _bucket_maps()
    rp, rc = pair_map.shape[0], cmp_map.shape[0]

    def out_spec(rows):
        return pl.BlockSpec((None, rows, LANES), lambda h: (h // NSA_GROUP, 0, h % NSA_GROUP))

    tp, cb, br = pl.pallas_call(
        _bias_kernel,
        grid=(NSA_HEADS,),
        in_specs=[pl.BlockSpec(memory_space=pltpu.SMEM),
                  pl.BlockSpec((rp, LANES), lambda h: (0, 0)),
                  pl.BlockSpec((rc, LANES), lambda h: (0, 0))],
        out_specs=[out_spec(rp), out_spec(rc), out_spec(16)],
        out_shape=[jax.ShapeDtypeStruct((NSA_KV_HEADS, rp, GQ), F32),
                   jax.ShapeDtypeStruct((NSA_KV_HEADS, rc, GQ), F32),
                   jax.ShapeDtypeStruct((NSA_KV_HEADS, 16, GQ), F32)],
        compiler_params=_cparams(("arbitrary",)),
        name="bias_tables",
    )(rel_bias, pair_map, cmp_map)
    return tp.reshape(NSA_KV_HEADS, N_PAIR_TABLES, KT, GQ), cb, br


def _kv_prep_kernel(ks_ref, vs_ref, kw_ref, vw_ref, g_ref, kso_ref, vso_ref, kwo_ref, vwo_ref):
    p = pl.program_id(2)

    def norm(x, g):
        ms = jnp.mean(x * x, axis=-1, keepdims=True)
        return x * lax.rsqrt(ms + EPS) * g

    lane = lax.broadcasted_iota(jnp.int32, (KT, LANES), 1)
    row = lax.broadcasted_iota(jnp.int32, (KT, LANES), 0)
    ones = jnp.where(lane < 2, 1.0, 0.0)
    blk = (KT // SEL_LEN) * p + jnp.right_shift(row, 6)
    onehot = jnp.where(lane == blk + (SEL_ROW0 - HEAD_DIM), 1.0, 0.0)
    ks = norm(ks_ref[...].astype(F32), g_ref[1:2, :])
    kw = norm(kw_ref[...].astype(F32), g_ref[2:3, :])
    kso_ref[...] = jnp.concatenate([ks, ones + onehot], axis=1).astype(BF16)
    kwo_ref[...] = jnp.concatenate([kw, ones], axis=1).astype(BF16)
    vso_ref[...] = vs_ref[...].astype(F32).T.astype(BF16)
    vwo_ref[...] = vw_ref[...].astype(F32).T.astype(BF16)


def kv_prep(proj, k_norm_g, B, T):
    npair = T // KT
    G = NSA_KV_HEADS
    assert SEL_LEN == 64 and SEL_ROW0 + T // SEL_LEN <= K_AUG

    def col(c0):
        return pl.BlockSpec((KT, HEAD_DIM), lambda b, g, p: (b * npair + p, c0 // HEAD_DIM + g))

    k_spec = pl.BlockSpec((None, None, KT, K_AUG), lambda b, g, p: (b, g, p, 0))
    v_spec = pl.BlockSpec((None, None, None, HEAD_DIM, KT), lambda b, g, p: (b, g, p, 0, 0))
    k_shape = jax.ShapeDtypeStruct((B, G, T, K_AUG), BF16)
    v_shape = jax.ShapeDtypeStruct((B, G, npair, HEAD_DIM, KT), BF16)
    return pl.pallas_call(
        _kv_prep_kernel,
        grid=(B, G, npair),
        in_specs=[col(COL_KSL), col(COL_VSL), col(COL_KWN), col(COL_VWN),
                  pl.BlockSpec((3, HEAD_DIM), lambda b, g, p: (0, 0))],
        out_specs=[k_spec, v_spec, k_spec, v_spec],
        out_shape=[k_shape, v_shape, k_shape, v_shape],
        compiler_params=_cparams(("parallel", "parallel", "parallel")),
        name="kv_prep",
    )(proj, proj, proj, proj, k_norm_g)


def _compress_kernel(ck_ref, cv_ref, pos_ref, w1_ref, w2_ref, g_ref, kc_ref, vct_ref):
    nchunk = ck_ref.shape[0]
    half = CMP_STRIDE * HEAD_DIM

    def mlp(x_ref, idx):
        x = x_ref[...].astype(F32)
        xa = (x + pos_ref[idx, :, 0:half]).astype(BF16)
        xb = (x + pos_ref[idx, :, half:2 * half]).astype(BF16)
        u = jnp.dot(xa, w1_ref[idx, 0:half, :], preferred_element_type=F32)
        v = jnp.dot(xb, w1_ref[idx, half:2 * half, :], preferred_element_type=F32)
        pre = u + pltpu.roll(v, nchunk - 1, 0)
        return jnp.dot(jax.nn.gelu(pre).astype(BF16), w2_ref[idx], preferred_element_type=F32)

    kc = mlp(ck_ref, 0)
    ms = jnp.mean(kc * kc, axis=-1, keepdims=True)
    kc_ref[...] = (kc * lax.rsqrt(ms + EPS) * g_ref[0:1, :]).astype(BF16)
    vct_ref[...] = mlp(cv_ref, 1).T.astype(BF16)


def compress_kv(proj, cmp_pos, cmp_w1, cmp_w2, k_norm_g, B, T):
    G = NSA_KV_HEADS
    nchunk = T // CMP_STRIDE
    half = CMP_STRIDE * HEAD_DIM
    ckv = proj[:, COL_KC:COL_KC + 2 * D_KV].reshape(B, nchunk, CMP_STRIDE, 2, G, HEAD_DIM)
    ckv = ckv.transpose(3, 0, 4, 1, 2, 5).reshape(2, B, G, nchunk, half)
    pos = cmp_pos.reshape(2, 1, CMP_LEN * HEAD_DIM)
    in_spec = pl.BlockSpec((None, None, nchunk, half), lambda b, g: (b, g, 0, 0))
    return pl.pallas_call(
        _compress_kernel,
        grid=(B, G),
        in_specs=[in_spec, in_spec,
                  pl.BlockSpec((2, 1, CMP_LEN * HEAD_DIM), lambda b, g: (0, 0, 0)),
                  pl.BlockSpec((2, CMP_LEN * HEAD_DIM, CMP_HIDDEN), lambda b, g: (0, 0, 0)),
                  pl.BlockSpec((2, CMP_HIDDEN, HEAD_DIM), lambda b, g: (0, 0, 0)),
                  pl.BlockSpec((3, HEAD_DIM), lambda b, g: (0, 0))],
        out_specs=[pl.BlockSpec((None, None, nchunk, HEAD_DIM), lambda b, g: (b, g, 0, 0)),
                   pl.BlockSpec((None, None, HEAD_DIM, nchunk), lambda b, g: (b, g, 0, 0))],
        out_shape=[jax.ShapeDtypeStruct((B, G, nchunk, HEAD_DIM), BF16),
                   jax.ShapeDtypeStruct((B, G, HEAD_DIM, nchunk), BF16)],
        compiler_params=_cparams(("parallel", "parallel")),
        name="compress_kv",
    )(ckv[0], ckv[1], pos, cmp_w1.astype(BF16), cmp_w2.astype(BF16), k_norm_g)


def _nsa_kernel(q_ref, gate_ref, kc_ref, vct_ref, ks_ref, vst_ref, kw_ref, vwt_ref,
                tp_ref, cbt_ref, brow_ref, qg_ref, ovl_ref, o_ref, acc_ref, qa_ref, cb_ref, *, top_n):
    i = pl.program_id(2)
    parity = jnp.bitwise_and(i, 1)
    pd = jnp.right_shift(i, 1)
    nchunk = kc_ref.shape[0]
    n_s = ovl_ref.shape[0]
    scale = HEAD_DIM ** -0.5 * LOG2E

    q = q_ref[...].astype(F32)
    qt = jnp.concatenate([q[:, h * HEAD_DIM:(h + 1) * HEAD_DIM].T for h in range(NSA_GROUP)], axis=1)
    ms = jnp.mean(qt * qt, axis=0, keepdims=True)
    q2t = (qt * lax.rsqrt(ms + EPS) * (qg_ref[...] * scale)).astype(BF16)
    qa_ref[0:HEAD_DIM, :] = q2t
    qa_ref[BIAS_ROW0:SEL_ROW0, :] = brow_ref[...].astype(BF16)
    qa_ref[SEL_ROW0:K_AUG, :] = jnp.zeros((K_AUG - SEL_ROW0, GQ), BF16)

    crow = lax.broadcasted_iota(jnp.int32, (nchunk, GQ), 0)
    near0 = CMP_PER_QB * i - CMP_PER_QB
    cb_ref[...] = jnp.where(crow < near0, cbt_ref[CMP_NEAR:CMP_NEAR + 1, :], NEG_INF)

    @pl.when(i == 0)
    def _():
        cb_ref[0:CMP_PER_QB, :] = cbt_ref[CMP_PER_QB:CMP_NEAR, :]

    @pl.when(i > 0)
    def _():
        cb_ref[pl.ds(pl.multiple_of(near0, CMP_PER_QB), CMP_NEAR), :] = cbt_ref[0:CMP_NEAR, :]

    lc = jnp.dot(kc_ref[...], q2t, preferred_element_type=F32) + cb_ref[...]
    mc = jnp.max(lc, axis=0, keepdims=True)
    ec = jnp.where(lc > 0.5 * NEG_INF, jnp.exp2(lc - mc), 0.0)
    pc = ec * (1.0 / jnp.maximum(jnp.sum(ec, axis=0, keepdims=True), 1e-30))
    o_cmp = jnp.dot(vct_ref[...], pc.astype(BF16), preferred_element_type=F32)

    pt = pc[:, 0:Q_BLOCK]
    for h in range(1, NSA_GROUP):
        pt = pt + pc[:, h * Q_BLOCK:(h + 1) * Q_BLOCK]
    ovl = ovl_ref[...]
    p1 = pt.astype(BF16)
    r1 = pt - p1.astype(F32)
    p2 = r1.astype(BF16)
    p3 = (r1 - p2.astype(F32)).astype(BF16)
    imp = (jnp.dot(ovl, p1, preferred_element_type=F32)
           + jnp.dot(ovl, p2, preferred_element_type=F32)
           + jnp.dot(ovl, p3, preferred_element_type=F32))
    sidx = lax.broadcasted_iota(jnp.int32, (n_s, Q_BLOCK), 0)
    rq = lax.broadcasted_iota(jnp.int32, (n_s, Q_BLOCK), 1)
    cur = (Q_BLOCK // SEL_LEN) * i + jnp.where(rq >= SEL_LEN, 1, 0)
    forced = (sidx == 0) | (sidx == cur) | (sidx == cur - 1)
    score = jnp.where(forced, FORCE_SCORE, jnp.where(sidx > cur, NEG_INF, imp))
    rank = jnp.zeros((n_s, Q_BLOCK), F32)
    for sp in range(n_s):
        row = score[sp:sp + 1, :]
        rank = rank + jnp.where(sidx > sp, jnp.where(row >= score, 1.0, 0.0), jnp.where(row > score, 1.0, 0.0))
    selb = jnp.where(rank < top_n, 0.0, NEG_INF)
    qa_ref[SEL_ROW0:SEL_ROW0 + n_s, :] = jnp.concatenate([selb] * NSA_GROUP, axis=1).astype(BF16)

    def pair_step(k_ref, vt_ref, p, table, carry):
        m, l = carry
        k = k_ref[pl.ds(pl.multiple_of(p * KT, KT), KT), :]
        s = jnp.dot(k, qa_ref[...], preferred_element_type=F32)
        if table is not None:
            s = s + table
        m_new = jnp.maximum(m, jnp.max(s, axis=0, keepdims=True))
        alpha = jnp.exp2(m - m_new)
        pe = jnp.exp2(s - m_new)
        l = alpha * l + jnp.sum(pe, axis=0, keepdims=True)
        acc_ref[...] = acc_ref[...] * alpha + jnp.dot(vt_ref[p], pe.astype(BF16), preferred_element_type=F32)
        return m_new, l

    def near_table(p):
        return tp_ref[3 * parity + (pd - p)]

    carry0 = (jnp.full((1, GQ), NEG_INF, F32), jnp.zeros((1, GQ), F32))

    acc_ref[...] = jnp.zeros_like(acc_ref)
    n_far = jnp.maximum(pd - 1, 0)
    carry = lax.fori_loop(0, n_far, lambda p, c: pair_step(ks_ref, vst_ref, p, None, c), carry0)
    _, l_sel = lax.fori_loop(n_far, pd + 1, lambda p, c: pair_step(ks_ref, vst_ref, p, near_table(p), c), carry)
    o_sel = acc_ref[...] * (1.0 / l_sel)

    acc_ref[...] = jnp.zeros_like(acc_ref)
    _, l_win = lax.fori_loop(0, jnp.minimum(pd, 2) + 1,
                             lambda t, c: pair_step(kw_ref, vwt_ref, pd - t, near_table(pd - t), c), carry0)
    o_win = acc_ref[...] * (1.0 / l_win)

    gt = jax.nn.sigmoid(gate_ref[...].astype(F32)).T
    first_group = pl.program_id(1) == 0

    def gate_row(br):
        rows = []
        for h in range(NSA_GROUP):
            lo = 3 * h + br
            hi = 3 * (NSA_GROUP + h) + br
            rows.append(jnp.where(first_group, gt[lo:lo + 1, :], gt[hi:hi + 1, :]))
        return jnp.concatenate(rows, axis=1)

    ot = gate_row(0) * o_cmp + gate_row(1) * o_sel + gate_row(2) * o_win
    for h in range(NSA_GROUP):
        o_ref[:, h * HEAD_DIM:(h + 1) * HEAD_DIM] = ot[:, h * Q_BLOCK:(h + 1) * Q_BLOCK].T.astype(o_ref.dtype)


def _overlap_t(T):
    n_c = T // CMP_STRIDE
    n_s = T // SEL_LEN
    c_start = jnp.arange(n_c) * CMP_STRIDE
    c_end = c_start + CMP_LEN - 1
    s_start = jnp.arange(n_s) * SEL_LEN
    ov = (c_start[None, :] < s_start[:, None] + SEL_LEN) & (c_end[None, :] >= s_start[:, None])
    ov = ov & (jnp.arange(n_c)[None, :] < n_c - 1)
    return ov.astype(BF16)


def nsa_attention(proj, q_norm_g, kc, vct, ks, vst, kw, vwt, tp, cbt, brow, B, T):
    nb = T // Q_BLOCK
    npair = T // KT
    G = NSA_KV_HEADS
    nchunk = T // CMP_STRIDE
    n_s = T // SEL_LEN
    top_n = min(SEL_TOPN, n_s)
    k_spec = pl.BlockSpec((None, None, T, K_AUG), lambda b, g, i: (b, g, 0, 0))
    v_spec = pl.BlockSpec((None, None, npair, HEAD_DIM, KT), lambda b, g, i: (b, g, 0, 0, 0))
    gate_blk = COL_GATE // LANES
    return pl.pallas_call(
        functools.partial(_nsa_kernel, top_n=top_n),
        grid=(B, G, nb),
        in_specs=[pl.BlockSpec((Q_BLOCK, GQ), lambda b, g, i: (b * nb + i, g)),
                  pl.BlockSpec((Q_BLOCK, LANES), lambda b, g, i: (b * nb + i, gate_blk)),
                  pl.BlockSpec((None, None, nchunk, HEAD_DIM), lambda b, g, i: (b, g, 0, 0)),
                  pl.BlockSpec((None, None, HEAD_DIM, nchunk), lambda b, g, i: (b, g, 0, 0)),
                  k_spec, v_spec, k_spec, v_spec,
                  pl.BlockSpec((None, N_PAIR_TABLES, KT, GQ), lambda b, g, i: (g, 0, 0, 0)),
                  pl.BlockSpec((None, CMP_NEAR + 8, GQ), lambda b, g, i: (g, 0, 0)),
                  pl.BlockSpec((None, 16, GQ), lambda b, g, i: (g, 0, 0)),
                  pl.BlockSpec((HEAD_DIM, 1), lambda b, g, i: (0, 0)),
                  pl.BlockSpec((n_s, nchunk), lambda b, g, i: (0, 0))],
        out_specs=pl.BlockSpec((Q_BLOCK, GQ), lambda b, g, i: (b * nb + i, g)),
        out_shape=jax.ShapeDtypeStruct((B * T, D_NSA), BF16),
        scratch_shapes=[pltpu.VMEM((HEAD_DIM, GQ), F32),
                        pltpu.VMEM((K_AUG, GQ), BF16),
                        pltpu.VMEM((nchunk, GQ), F32)],
        compiler_params=_cparams(("parallel", "parallel", "arbitrary")),
        name="nsa_attention",
    )(proj, proj, kc, vct, ks, vst, kw, vwt, tp, cbt, brow, q_norm_g.reshape(HEAD_DIM, 1), _overlap_t(T))


def _ret_tables(T):
    H, C = RET_HEADS, RET_CHUNK
    half = RET_HEAD_DIM // 2
    pos = jnp.arange(T, dtype=F32)
    freqs = ROPE_BASE ** (-jnp.arange(half, dtype=F32) / half)
    ang = pos[:, None] * freqs[None, :]
    log_gamma = jnp.log(1.0 - 2.0 ** (-5.0 - jnp.arange(H, dtype=F32)))
    n = jnp.arange(C, dtype=F32)
    diff = n[:, None] - n[None, :]
    decay_in = jnp.where(diff >= 0, jnp.exp(jnp.maximum(diff, 0.0)[None] * log_gamma[:, None, None]), 0.0)
    q_decay = jnp.exp((n + 1.0)[None] * log_gamma[:, None])
    k_decay = jnp.exp((C - 1.0 - n)[None] * log_gamma[:, None])
    chunk_decay = jnp.exp(C * log_gamma)
    qk_decay = jnp.stack([q_decay, k_decay], axis=1)[..., None]
    return jnp.cos(ang), jnp.sin(ang), decay_in, qk_decay, chunk_decay


def _retention_kernel(cd_ref, q_ref, k_ref, v_ref, g_ref, cos_ref, sin_ref, din_ref, qkd_ref, gn_ref,
                      o_ref, state_ref):
    c = pl.program_id(1)
    d = RET_HEAD_DIM
    half = d // 2

    @pl.when(c == 0)
    def _():
        state_ref[...] = jnp.zeros_like(state_ref)

    cos = cos_ref[...]
    sin = sin_ref[...]

    def rope(x):
        x1, x2 = x[:, :half], x[:, half:]
        return jnp.concatenate([x1 * cos - x2 * sin, x2 * cos + x1 * sin], axis=1)

    for h in range(RET_HEADS):
        sl = slice(h * d, (h + 1) * d)
        qr = rope(q_ref[:, sl].astype(F32))
        kr = rope(k_ref[:, sl].astype(F32)) * (d ** -0.5)
        v = v_ref[:, sl]
        qb = qr.astype(BF16)
        inner = lax.dot_general(qb, kr.astype(BF16), (((1,), (1,)), ((), ())), preferred_element_type=F32)
        inner = (inner * din_ref[h]).astype(BF16)
        state = state_ref[h]
        cross = jnp.dot((qr * qkd_ref[h, 0]).astype(BF16), state.astype(BF16), preferred_element_type=F32)
        o = jnp.dot(inner, v, preferred_element_type=F32) + cross
        kdt = (kr * qkd_ref[h, 1]).T.astype(BF16)
        state_ref[h] = state * cd_ref[h] + jnp.dot(kdt, v, preferred_element_type=F32)
        mu = jnp.mean(o, axis=-1, keepdims=True)
        var = jnp.mean(jnp.square(o - mu), axis=-1, keepdims=True)
        on = (o - mu) * lax.rsqrt(var + EPS) * gn_ref[:, sl]
        o_ref[:, sl] = (jax.nn.silu(g_ref[:, sl].astype(F32)) * on).astype(o_ref.dtype)


def retention(proj, gn_g, B, T):
    nc = T // RET_CHUNK
    cos, sin, decay_in, qk_decay, chunk_decay = _ret_tables(T)
    cblk = COL_RET // D_RET

    def seg(k):
        return pl.BlockSpec((RET_CHUNK, D_RET), lambda b, c: (b * nc + c, cblk + k))

    rope_spec = pl.BlockSpec((RET_CHUNK, RET_HEAD_DIM // 2), lambda b, c: (c, 0))
    return pl.pallas_call(
        _retention_kernel,
        grid=(B, nc),
        in_specs=[pl.BlockSpec(memory_space=pltpu.SMEM),
                  seg(0), seg(1), seg(2), seg(3), rope_spec, rope_spec,
                  pl.BlockSpec((RET_HEADS, RET_CHUNK, RET_CHUNK), lambda b, c: (0, 0, 0)),
                  pl.BlockSpec((RET_HEADS, 2, RET_CHUNK, 1), lambda b, c: (0, 0, 0, 0)),
                  pl.BlockSpec((1, D_RET), lambda b, c: (0, 0))],
        out_specs=pl.BlockSpec((RET_CHUNK, D_RET), lambda b, c: (b * nc + c, 0)),
        out_shape=jax.ShapeDtypeStruct((B * T, D_RET), BF16),
        scratch_shapes=[pltpu.VMEM((RET_HEADS, RET_HEAD_DIM, RET_HEAD_DIM), F32)],
        compiler_params=_cparams(("parallel", "arbitrary")),
        name="retention",
    )(chunk_decay, proj, proj, proj, proj, cos, sin, decay_in, qk_decay, gn_g.reshape(1, D_RET))


def kernel(x, norm1_g, w_in, nsa_q_norm_g, nsa_k_norm_g, cmp_pos, cmp_w1, cmp_w2, ret_gn_g,
           w_up_nsa, w_up_ret, w_out, norm2_g, w_ff1, w_ff2, rel_bias):
    B, T, D = x.shape
    depth = w_in.shape[0]
    xf = x.reshape(B * T, D)
    tp, cbt, brow = bias_tables(rel_bias)
    w_up_nsa_b, w_up_ret_b = w_up_nsa.astype(BF16), w_up_ret.astype(BF16)
    w_out_b, w_ff2_b = w_out.astype(BF16), w_ff2.astype(BF16)
    w_in_t = jnp.swapaxes(w_in, 1, 2)
    for l in range(depth):
        h = rmsnorm_bf16(xf, norm1_g[l])
        proj = matmul_wcast(h, w_in_t, l, n_out=D_PROJ, transposed=True, n_plain=COL_RET // 1024,
                            seg_start=N_REAL_A, name="in_proj")
        ks, vst, kw, vwt = kv_prep(proj, nsa_k_norm_g[l], B, T)
        kc, vct = compress_kv(proj, cmp_pos[l], cmp_w1[l], cmp_w2[l], nsa_k_norm_g[l], B, T)
        o_a = nsa_attention(proj, nsa_q_norm_g[l], kc, vct, ks, vst, kw, vwt, tp, cbt, brow, B, T)
        o_r = retention(proj, ret_gn_g[l], B, T)
        y = merge_up_proj(o_a, w_up_nsa_b, o_r, w_up_ret_b, l, proj)
        xf = matmul_residual(y, w_out_b, l, xf, name="out_proj")
        h2 = rmsnorm_bf16(xf, norm2_g[l])
        a = matmul_wcast(h2, w_ff1, l, n_out=D_FF, relu2=True, name="ffn_up")
        xf = matmul_residual(a, w_ff2_b, l, xf, name="ffn_down")
    return xf.reshape(B, T, D)
```

```python
import functools
import math

import jax
import jax.numpy as jnp
from jax import lax
from jax.experimental import pallas as pl
from jax.experimental.pallas import tpu as pltpu

F32 = jnp.float32
BF16 = jnp.bfloat16

D_MODEL = 4096
HEAD_DIM = 128
D_NSA = D_MODEL // 2
NSA_HEADS = D_NSA // HEAD_DIM
NSA_KV_HEADS = 2
NSA_GROUP = NSA_HEADS // NSA_KV_HEADS
D_KV = NSA_KV_HEADS * HEAD_DIM
CMP_LEN = 32
CMP_STRIDE = 16
CMP_HIDDEN = 256
SEL_LEN = 64
SEL_TOPN = 16
WINDOW = 512
Q_BLOCK = 128
D_RET = D_MODEL // 2
RET_HEADS = 8
RET_HEAD_DIM = D_RET // RET_HEADS
RET_CHUNK = 128
D_FF = 4 * D_MODEL
N_BUCKETS = 32
MAX_DISTANCE = 128
ROPE_BASE = 10000.0
EPS = 1e-6
NEG_INF = -1e30
FORCE_SCORE = 1e9
LOG2E = 1.4426950408889634

COL_Q = 0
COL_KC = D_NSA
COL_KSL = COL_KC + 2 * D_KV
COL_VSL = COL_KSL + D_KV
COL_KWN = COL_VSL + D_KV
COL_VWN = COL_KWN + D_KV
COL_GATE = COL_VWN + D_KV
N_GATE = 3 * NSA_HEADS
COL_RET = 4096
COL_MERGE_A = COL_RET + 4 * D_RET
COL_MERGE_R = COL_MERGE_A + D_MODEL
D_PROJ = COL_MERGE_R + D_MODEL
N_REAL_A = D_NSA + 6 * D_KV + N_GATE

LANES = 128
GQ = NSA_GROUP * Q_BLOCK
KT = 2 * Q_BLOCK
K_AUG = HEAD_DIM + LANES
BIAS_ROW0 = HEAD_DIM
SEL_ROW0 = HEAD_DIM + 16
CMP_PER_QB = Q_BLOCK // CMP_STRIDE
CMP_NEAR = 2 * CMP_PER_QB
VMEM_LIMIT = 60 * 1024 * 1024


def _cparams(sem, vmem=VMEM_LIMIT):
    return pltpu.CompilerParams(dimension_semantics=sem, vmem_limit_bytes=vmem)


def _rmsnorm_kernel(x_ref, g_ref, o_ref):
    x = x_ref[...]
    ms = jnp.mean(x * x, axis=-1, keepdims=True)
    o_ref[...] = (x * lax.rsqrt(ms + EPS) * g_ref[...]).astype(o_ref.dtype)


def rmsnorm_bf16(x, g, tm=256):
    n, d = x.shape
    return pl.pallas_call(
        _rmsnorm_kernel,
        grid=(n // tm,),
        in_specs=[pl.BlockSpec((tm, d), lambda i: (i, 0)),
                  pl.BlockSpec((1, d), lambda i: (0, 0))],
        out_specs=pl.BlockSpec((tm, d), lambda i: (i, 0)),
        out_shape=jax.ShapeDtypeStruct((n, d), BF16),
        compiler_params=_cparams(("parallel",)),
        name="rmsnorm",
    )(x, g.reshape(1, d))


W_CAST_VREGS = 256


def _mm_ws_kernel(a_ref, w_hbm, *rest, layer, transposed, n_plain, seg_start, relu2, n_side):
    side_refs, rest = rest[:n_side], rest[n_side:]
    o_ref, rest = rest[0], rest[1:]
    side_o_refs, (stage_ref, wb_ref, sem) = rest[:n_side], rest[n_side:]
    for src, dst in zip(side_refs, side_o_refs):
        dst[...] = src[...].astype(BF16)
    j = pl.program_id(0)
    i = pl.program_id(1)
    nj = pl.num_programs(0)
    tn = o_ref.shape[1]

    def w_copy(jj):
        if transposed:
            start = jnp.where(jj < n_plain, jj * tn, seg_start + (jj - n_plain) * tn)
            src = w_hbm.at[layer, pl.ds(pl.multiple_of(start, 16), tn), :]
        else:
            src = w_hbm.at[layer, :, pl.ds(pl.multiple_of(jj * tn, tn), tn)]
        return pltpu.make_async_copy(src, stage_ref, sem)

    @pl.when(i == 0)
    def _():
        @pl.when(j == 0)
        def _():
            w_copy(j).start()

        w_copy(j).wait()
        rows, cols = stage_ref.shape
        step = min(rows, W_CAST_VREGS * 8 * LANES // cols)
        assert rows % step == 0

        def cast(r, _):
            sl = pl.ds(pl.multiple_of(r * step, step), step)
            wb_ref[sl, :] = stage_ref[sl, :].astype(BF16)
            return 0

        lax.fori_loop(0, rows // step, cast, 0)

        @pl.when(j + 1 < nj)
        def _():
            w_copy(j + 1).start()

    if transposed:
        acc = lax.dot_general(a_ref[...], wb_ref[...], (((1,), (1,)), ((), ())), preferred_element_type=F32)
    else:
        acc = jnp.dot(a_ref[...], wb_ref[...], preferred_element_type=F32)
    if relu2:
        acc = jnp.square(jnp.maximum(acc, 0.0))
    o_ref[...] = acc.astype(o_ref.dtype)


def matmul_wcast(a, w3, layer, *, n_out, transposed=False, n_plain=None, seg_start=0, relu2=False,
                 sides=(), tm=1024, tn=1024, name="mm"):
    m, k = a.shape
    tm = min(tm, m)
    nj, ni = n_out // tn, m // tm
    n_plain = nj if n_plain is None else n_plain
    assert transposed or n_plain == nj
    stage_shape = (tn, k) if transposed else (k, tn)
    in_specs = [pl.BlockSpec((tm, k), lambda j, i: (i, 0)),
                pl.BlockSpec(memory_space=pl.ANY)]
    out_specs = [pl.BlockSpec((tm, tn), lambda j, i: (i, j))]
    out_shape = [jax.ShapeDtypeStruct((m, n_out), BF16)]
    args = [a, w3]
    side_in, side_out = [], []
    for arr, slab in sides:
        _, r, c = arr.shape
        n_slab = r // slab
        assert n_slab * slab == r and n_slab <= nj * ni and slab % 16 == 0
        side_in.append(pl.BlockSpec((None, slab, c),
                                    lambda j, i, n_slab=n_slab: (layer, jnp.minimum(j * ni + i, n_slab - 1), 0)))
        side_out.append(pl.BlockSpec((slab, c), lambda j, i, n_slab=n_slab: (jnp.minimum(j * ni + i, n_slab - 1), 0)))
        out_shape.append(jax.ShapeDtypeStruct((r, c), BF16))
        args.append(arr)
    in_specs += side_in
    out_specs += side_out
    outs = pl.pallas_call(
        functools.partial(_mm_ws_kernel, layer=layer, transposed=transposed, n_plain=n_plain,
                          seg_start=seg_start, relu2=relu2, n_side=len(sides)),
        grid=(nj, ni),
        in_specs=in_specs,
        out_specs=out_specs,
        out_shape=out_shape,
        scratch_shapes=[pltpu.VMEM(stage_shape, F32),
                        pltpu.VMEM(stage_shape, BF16),
                        pltpu.SemaphoreType.DMA(())],
        compiler_params=_cparams(("arbitrary", "arbitrary")),
        name=name,
    )(*args)
    return outs[0] if not sides else outs


def _mm_res_kernel(a_ref, w_ref, r_ref, o_ref):
    k = pl.program_id(2)
    acc = jnp.dot(a_ref[...], w_ref[...], preferred_element_type=F32)

    @pl.when(k == 0)
    def _():
        o_ref[...] = r_ref[...] + acc

    @pl.when(k > 0)
    def _():
        o_ref[...] += acc


def matmul_residual(a, w, res, *, layer=None, tm=1024, tn=1024, tk=4096, name="mm_res"):
    m, k = a.shape
    n = w.shape[-1]
    tm, tn, tk = min(tm, m), min(tn, n), min(tk, k)
    if layer is None:
        w_spec = pl.BlockSpec((tk, tn), lambda i, j, kk: (kk, j))
    else:
        w_spec = pl.BlockSpec((None, tk, tn), lambda i, j, kk: (layer, kk, j))
    return pl.pallas_call(
        _mm_res_kernel,
        grid=(m // tm, n // tn, k // tk),
        in_specs=[pl.BlockSpec((tm, tk), lambda i, j, kk: (i, kk)),
                  w_spec,
                  pl.BlockSpec((tm, tn), lambda i, j, kk: (i, j), pipeline_mode=pl.Buffered(1))],
        out_specs=pl.BlockSpec((tm, tn), lambda i, j, kk: (i, j)),
        out_shape=jax.ShapeDtypeStruct((m, n), F32),
        compiler_params=_cparams(("parallel", "arbitrary", "arbitrary")),
        name=name,
    )(a, w, res)


def _mm_merge_kernel(a1_ref, w1_ref, a2_ref, w2_ref, g1_ref, g2_ref, o_ref):
    u1 = jnp.dot(a1_ref[...], w1_ref[...], preferred_element_type=F32)
    u2 = jnp.dot(a2_ref[...], w2_ref[...], preferred_element_type=F32)
    y = (jax.nn.sigmoid(g1_ref[...].astype(F32)) * u1
         + jax.nn.sigmoid(g2_ref[...].astype(F32)) * u2)
    o_ref[...] = y.astype(o_ref.dtype)


def merge_up_proj(o_a, w_a, o_r, w_r, proj, *, tm=1024, tn=1024):
    m, k = o_a.shape
    n = w_a.shape[1]
    tm = min(tm, m)
    ca, cr = COL_MERGE_A // tn, COL_MERGE_R // tn
    return pl.pallas_call(
        _mm_merge_kernel,
        grid=(m // tm, n // tn),
        in_specs=[pl.BlockSpec((tm, k), lambda i, j: (i, 0)),
                  pl.BlockSpec((k, tn), lambda i, j: (0, j)),
                  pl.BlockSpec((tm, k), lambda i, j: (i, 0)),
                  pl.BlockSpec((k, tn), lambda i, j: (0, j)),
                  pl.BlockSpec((tm, tn), lambda i, j: (i, ca + j)),
                  pl.BlockSpec((tm, tn), lambda i, j: (i, cr + j))],
        out_specs=pl.BlockSpec((tm, tn), lambda i, j: (i, j)),
        out_shape=jax.ShapeDtypeStruct((m, n), BF16),
        compiler_params=_cparams(("parallel", "arbitrary")),
        name="merge_up",
    )(o_a, w_a, o_r, w_r, proj, proj)


def _t5_bucket(dist):
    n = jnp.maximum(dist, 0)
    max_exact = N_BUCKETS // 2
    large = max_exact + (jnp.log(jnp.maximum(n, 1).astype(F32) / max_exact)
                         / math.log(MAX_DISTANCE / max_exact) * (N_BUCKETS - max_exact)).astype(jnp.int32)
    large = jnp.minimum(large, N_BUCKETS - 1)
    return jnp.where(n < max_exact, n, large)


N_PAIR_TABLES = 6


def _bucket_maps():
    kk = jnp.arange(Q_BLOCK)[:, None]
    r = jnp.arange(Q_BLOCK)[None, :]

    def tile(delta):
        dist = delta * Q_BLOCK + r - kk
        return jnp.where((dist >= 0) & (dist < WINDOW), _t5_bucket(dist), -1)

    neg = jnp.full((Q_BLOCK, Q_BLOCK), -1)
    far = jnp.full((Q_BLOCK, Q_BLOCK), N_BUCKETS - 1)
    pairs = [(tile(0), neg), (far, tile(1)), (tile(4), far),
             (tile(1), tile(0)), (far, far), (neg, tile(4))]
    pair_map = jnp.concatenate([jnp.concatenate(p, axis=0) for p in pairs], axis=0).astype(jnp.int32)
    mrow = jnp.arange(CMP_NEAR)[:, None]
    dist_c = r - CMP_STRIDE * (mrow - CMP_PER_QB) - (CMP_LEN - 1)
    cmp_rows = jnp.where(dist_c >= 0, _t5_bucket(dist_c), -1)
    cmp_map = jnp.concatenate([cmp_rows, jnp.full((8, Q_BLOCK), N_BUCKETS - 1)], axis=0).astype(jnp.int32)
    return pair_map, cmp_map


def _bias_kernel(relb_ref, pmap_ref, cmap_ref, tp_ref, cb_ref, br_ref):
    h = pl.program_id(0)
    far = relb_ref[N_BUCKETS - 1, h]

    def lookup(bmap, offset):
        out = jnp.full(bmap.shape, NEG_INF, F32)
        for b in range(N_BUCKETS):
            out = jnp.where(bmap == b, (relb_ref[b, h] - offset) * LOG2E, out)
        return out

    tp_ref[...] = lookup(pmap_ref[...], far)
    cb_ref[...] = lookup(cmap_ref[...], 0.0)
    full = jnp.full(br_ref.shape, far * LOG2E, F32)
    hi = full.astype(BF16).astype(F32)
    row = lax.broadcasted_iota(jnp.int32, br_ref.shape, 0)
    br_ref[...] = jnp.where(row == 0, hi, jnp.where(row == 1, full - hi, 0.0))


def bias_tables(rel_bias):
    pair_map, cmp_map = _bucket_maps()
    rp, rc = pair_map.shape[0], cmp_map.shape[0]

    def out_spec(rows):
        return pl.BlockSpec((None, rows, LANES), lambda h: (h // NSA_GROUP, 0, h % NSA_GROUP))

    tp, cb, br = pl.pallas_call(
        _bias_kernel,
        grid=(NSA_HEADS,),
        in_specs=[pl.BlockSpec(memory_space=pltpu.SMEM),
                  pl.BlockSpec((rp, LANES), lambda h: (0, 0)),
                  pl.BlockSpec((rc, LANES), lambda h: (0, 0))],
        out_specs=[out_spec(rp), out_spec(rc), out_spec(16)],
        out_shape=[jax.ShapeDtypeStruct((NSA_KV_HEADS, rp, GQ), F32),
                   jax.ShapeDtypeStruct((NSA_KV_HEADS, rc, GQ), F32),
                   jax.ShapeDtypeStruct((NSA_KV_HEADS, 16, GQ), F32)],
        compiler_params=_cparams(("arbitrary",)),
        name="bias_tables",
    )(rel_bias, pair_map, cmp_map)
    return tp.reshape(NSA_KV_HEADS, N_PAIR_TABLES, KT, GQ), cb, br


def _kv_prep_kernel(ks_ref, vs_ref, kw_ref, vw_ref, g_ref, kso_ref, vso_ref, kwo_ref, vwo_ref):
    p = pl.program_id(2)

    def norm(x, g):
        ms = jnp.mean(x * x, axis=-1, keepdims=True)
        return x * lax.rsqrt(ms + EPS) * g

    lane = lax.broadcasted_iota(jnp.int32, (KT, LANES), 1)
    row = lax.broadcasted_iota(jnp.int32, (KT, LANES), 0)
    ones = jnp.where(lane < 2, 1.0, 0.0)
    blk = (KT // SEL_LEN) * p + jnp.right_shift(row, 6)
    onehot = jnp.where(lane == blk + (SEL_ROW0 - HEAD_DIM), 1.0, 0.0)
    ks = norm(ks_ref[...].astype(F32), g_ref[1:2, :])
    kw = norm(kw_ref[...].astype(F32), g_ref[2:3, :])
    kso_ref[...] = jnp.concatenate([ks, ones + onehot], axis=1).astype(BF16)
    kwo_ref[...] = jnp.concatenate([kw, ones], axis=1).astype(BF16)
    vso_ref[...] = vs_ref[...].astype(F32).T.astype(BF16)
    vwo_ref[...] = vw_ref[...].astype(F32).T.astype(BF16)


def kv_prep(proj, k_norm_g, B, T):
    npair = T // KT
    G = NSA_KV_HEADS
    assert SEL_LEN == 64 and SEL_ROW0 + T // SEL_LEN <= K_AUG

    def col(c0):
        return pl.BlockSpec((KT, HEAD_DIM), lambda b, g, p: (b * npair + p, c0 // HEAD_DIM + g))

    k_spec = pl.BlockSpec((None, None, KT, K_AUG), lambda b, g, p: (b, g, p, 0))
    v_spec = pl.BlockSpec((None, None, None, HEAD_DIM, KT), lambda b, g, p: (b, g, p, 0, 0))
    k_shape = jax.ShapeDtypeStruct((B, G, T, K_AUG), BF16)
    v_shape = jax.ShapeDtypeStruct((B, G, npair, HEAD_DIM, KT), BF16)
    return pl.pallas_call(
        _kv_prep_kernel,
        grid=(B, G, npair),
        in_specs=[col(COL_KSL), col(COL_VSL), col(COL_KWN), col(COL_VWN),
                  pl.BlockSpec((3, HEAD_DIM), lambda b, g, p: (0, 0))],
        out_specs=[k_spec, v_spec, k_spec, v_spec],
        out_shape=[k_shape, v_shape, k_shape, v_shape],
        compiler_params=_cparams(("parallel", "parallel", "parallel")),
        name="kv_prep",
    )(proj, proj, proj, proj, k_norm_g)


def _compress_kernel(ck_ref, cv_ref, pos_ref, w1_ref, w2_ref, g_ref, kc_ref, vct_ref):
    nchunk = ck_ref.shape[0]
    half = CMP_STRIDE * HEAD_DIM

    def mlp(x_ref, idx):
        x = x_ref[...].astype(F32)
        xa = (x + pos_ref[idx, :, 0:half]).astype(BF16)
        xb = (x + pos_ref[idx, :, half:2 * half]).astype(BF16)
        u = jnp.dot(xa, w1_ref[idx, 0:half, :], preferred_element_type=F32)
        v = jnp.dot(xb, w1_ref[idx, half:2 * half, :], preferred_element_type=F32)
        pre = u + pltpu.roll(v, nchunk - 1, 0)
        return jnp.dot(jax.nn.gelu(pre).astype(BF16), w2_ref[idx], preferred_element_type=F32)

    kc = mlp(ck_ref, 0)
    ms = jnp.mean(kc * kc, axis=-1, keepdims=True)
    kc_ref[...] = (kc * lax.rsqrt(ms + EPS) * g_ref[0:1, :]).astype(BF16)
    vct_ref[...] = mlp(cv_ref, 1).T.astype(BF16)


def compress_kv(proj, cmp_pos, cmp_w1, cmp_w2, k_norm_g, B, T):
    G = NSA_KV_HEADS
    nchunk = T // CMP_STRIDE
    half = CMP_STRIDE * HEAD_DIM
    ckv = proj[:, COL_KC:COL_KC + 2 * D_KV].reshape(B, nchunk, CMP_STRIDE, 2, G, HEAD_DIM)
    ckv = ckv.transpose(3, 0, 4, 1, 2, 5).reshape(2, B, G, nchunk, half)
    pos = cmp_pos.reshape(2, 1, CMP_LEN * HEAD_DIM)
    in_spec = pl.BlockSpec((None, None, nchunk, half), lambda b, g: (b, g, 0, 0))
    return pl.pallas_call(
        _compress_kernel,
        grid=(B, G),
        in_specs=[in_spec, in_spec,
                  pl.BlockSpec((2, 1, CMP_LEN * HEAD_DIM), lambda b, g: (0, 0, 0)),
                  pl.BlockSpec((2, CMP_LEN * HEAD_DIM, CMP_HIDDEN), lambda b, g: (0, 0, 0)),
                  pl.BlockSpec((2, CMP_HIDDEN, HEAD_DIM), lambda b, g: (0, 0, 0)),
                  pl.BlockSpec((3, HEAD_DIM), lambda b, g: (0, 0))],
        out_specs=[pl.BlockSpec((None, None, nchunk, HEAD_DIM), lambda b, g: (b, g, 0, 0)),
                   pl.BlockSpec((None, None, HEAD_DIM, nchunk), lambda b, g: (b, g, 0, 0))],
        out_shape=[jax.ShapeDtypeStruct((B, G, nchunk, HEAD_DIM), BF16),
                   jax.ShapeDtypeStruct((B, G, HEAD_DIM, nchunk), BF16)],
        compiler_params=_cparams(("parallel", "parallel")),
        name="compress_kv",
    )(ckv[0], ckv[1], pos, cmp_w1.astype(BF16), cmp_w2.astype(BF16), k_norm_g)


def _nsa_kernel(q_ref, gate_ref, kc_ref, vct_ref, ks_ref, vst_ref, kw_ref, vwt_ref,
                tp_ref, cbt_ref, brow_ref, qg_ref, ovl_ref, o_ref, acc_ref, qa_ref, cb_ref, *, top_n):
    i = pl.program_id(2)
    parity = jnp.bitwise_and(i, 1)
    pd = jnp.right_shift(i, 1)
    nchunk = kc_ref.shape[0]
    n_s = ovl_ref.shape[0]
    scale = HEAD_DIM ** -0.5 * LOG2E

    q = q_ref[...].astype(F32)
    qt = jnp.concatenate([q[:, h * HEAD_DIM:(h + 1) * HEAD_DIM].T for h in range(NSA_GROUP)], axis=1)
    ms = jnp.mean(qt * qt, axis=0, keepdims=True)
    q2t = (qt * lax.rsqrt(ms + EPS) * (qg_ref[...] * scale)).astype(BF16)
    qa_ref[0:HEAD_DIM, :] = q2t
    qa_ref[BIAS_ROW0:SEL_ROW0, :] = brow_ref[...].astype(BF16)
    qa_ref[SEL_ROW0:K_AUG, :] = jnp.zeros((K_AUG - SEL_ROW0, GQ), BF16)

    crow = lax.broadcasted_iota(jnp.int32, (nchunk, GQ), 0)
    near0 = CMP_PER_QB * i - CMP_PER_QB
    cb_ref[...] = jnp.where(crow < near0, cbt_ref[CMP_NEAR:CMP_NEAR + 1, :], NEG_INF)

    @pl.when(i == 0)
    def _():
        cb_ref[0:CMP_PER_QB, :] = cbt_ref[CMP_PER_QB:CMP_NEAR, :]

    @pl.when(i > 0)
    def _():
        cb_ref[pl.ds(pl.multiple_of(near0, CMP_PER_QB), CMP_NEAR), :] = cbt_ref[0:CMP_NEAR, :]

    lc = jnp.dot(kc_ref[...], q2t, preferred_element_type=F32) + cb_ref[...]
    mc = jnp.max(lc, axis=0, keepdims=True)
    ec = jnp.where(lc > 0.5 * NEG_INF, jnp.exp2(lc - mc), 0.0)
    pc = ec * (1.0 / jnp.maximum(jnp.sum(ec, axis=0, keepdims=True), 1e-30))
    o_cmp = jnp.dot(vct_ref[...], pc.astype(BF16), preferred_element_type=F32)

    pt = pc[:, 0:Q_BLOCK]
    for h in range(1, NSA_GROUP):
        pt = pt + pc[:, h * Q_BLOCK:(h + 1) * Q_BLOCK]
    ovl = ovl_ref[...]
    p1 = pt.astype(BF16)
    r1 = pt - p1.astype(F32)
    p2 = r1.astype(BF16)
    p3 = (r1 - p2.astype(F32)).astype(BF16)
    imp = (jnp.dot(ovl, p1, preferred_element_type=F32)
           + jnp.dot(ovl, p2, preferred_element_type=F32)
           + jnp.dot(ovl, p3, preferred_element_type=F32))
    sidx = lax.broadcasted_iota(jnp.int32, (n_s, Q_BLOCK), 0)
    rq = lax.broadcasted_iota(jnp.int32, (n_s, Q_BLOCK), 1)
    cur = (Q_BLOCK // SEL_LEN) * i + jnp.where(rq >= SEL_LEN, 1, 0)
    forced = (sidx == 0) | (sidx == cur) | (sidx == cur - 1)
    score = jnp.where(forced, FORCE_SCORE, jnp.where(sidx > cur, NEG_INF, imp))
    rank = jnp.zeros((n_s, Q_BLOCK), F32)
    for sp in range(n_s):
        row = score[sp:sp + 1, :]
        rank = rank + jnp.where(sidx > sp, jnp.where(row >= score, 1.0, 0.0), jnp.where(row > score, 1.0, 0.0))
    selb = jnp.where(rank < top_n, 0.0, NEG_INF)
    qa_ref[SEL_ROW0:SEL_ROW0 + n_s, :] = jnp.concatenate([selb] * NSA_GROUP, axis=1).astype(BF16)

    def pair_step(k_ref, vt_ref, p, table, carry):
        m, l = carry
        k = k_ref[pl.ds(pl.multiple_of(p * KT, KT), KT), :]
        s = jnp.dot(k, qa_ref[...], preferred_element_type=F32)
        if table is not None:
            s = s + table
        m_new = jnp.maximum(m, jnp.max(s, axis=0, keepdims=True))
        alpha = jnp.exp2(m - m_new)
        pe = jnp.exp2(s - m_new)
        l = alpha * l + jnp.sum(pe, axis=0, keepdims=True)
        acc_ref[...] = acc_ref[...] * alpha + jnp.dot(vt_ref[p], pe.astype(BF16), preferred_element_type=F32)
        return m_new, l

    def near_table(p):
        return tp_ref[3 * parity + (pd - p)]

    carry0 = (jnp.full((1, GQ), NEG_INF, F32), jnp.zeros((1, GQ), F32))

    acc_ref[...] = jnp.zeros_like(acc_ref)
    n_far = jnp.maximum(pd - 1, 0)
    carry = lax.fori_loop(0, n_far, lambda p, c: pair_step(ks_ref, vst_ref, p, None, c), carry0)
    _, l_sel = lax.fori_loop(n_far, pd + 1, lambda p, c: pair_step(ks_ref, vst_ref, p, near_table(p), c), carry)
    o_sel = acc_ref[...] * (1.0 / l_sel)

    acc_ref[...] = jnp.zeros_like(acc_ref)
    _, l_win = lax.fori_loop(0, jnp.minimum(pd, 2) + 1,
                             lambda t, c: pair_step(kw_ref, vwt_ref, pd - t, near_table(pd - t), c), carry0)
    o_win = acc_ref[...] * (1.0 / l_win)

    gt = jax.nn.sigmoid(gate_ref[...].astype(F32)).T
    first_group = pl.program_id(1) == 0

    def gate_row(br):
        rows = []
        for h in range(NSA_GROUP):
            lo = 3 * h + br
            hi = 3 * (NSA_GROUP + h) + br
            rows.append(jnp.where(first_group, gt[lo:lo + 1, :], gt[hi:hi + 1, :]))
        return jnp.concatenate(rows, axis=1)

    ot = gate_row(0) * o_cmp + gate_row(1) * o_sel + gate_row(2) * o_win
    for h in range(NSA_GROUP):
        o_ref[:, h * HEAD_DIM:(h + 1) * HEAD_DIM] = ot[:, h * Q_BLOCK:(h + 1) * Q_BLOCK].T.astype(o_ref.dtype)


def _overlap_t(T):
    n_c = T // CMP_STRIDE
    n_s = T // SEL_LEN
    c_start = jnp.arange(n_c) * CMP_STRIDE
    c_end = c_start + CMP_LEN - 1
    s_start = jnp.arange(n_s) * SEL_LEN
    ov = (c_start[None, :] < s_start[:, None] + SEL_LEN) & (c_end[None, :] >= s_start[:, None])
    ov = ov & (jnp.arange(n_c)[None, :] < n_c - 1)
    return ov.astype(BF16)


def nsa_attention(proj, q_norm_g, kc, vct, ks, vst, kw, vwt, tp, cbt, brow, B, T):
    nb = T // Q_BLOCK
    npair = T // KT
    G = NSA_KV_HEADS
    nchunk = T // CMP_STRIDE
    n_s = T // SEL_LEN
    top_n = min(SEL_TOPN, n_s)
    k_spec = pl.BlockSpec((None, None, T, K_AUG), lambda b, g, i: (b, g, 0, 0))
    v_spec = pl.BlockSpec((None, None, npair, HEAD_DIM, KT), lambda b, g, i: (b, g, 0, 0, 0))
    gate_blk = COL_GATE // LANES
    return pl.pallas_call(
        functools.partial(_nsa_kernel, top_n=top_n),
        grid=(B, G, nb),
        in_specs=[pl.BlockSpec((Q_BLOCK, GQ), lambda b, g, i: (b * nb + i, g)),
                  pl.BlockSpec((Q_BLOCK, LANES), lambda b, g, i: (b * nb + i, gate_blk)),
                  pl.BlockSpec((None, None, nchunk, HEAD_DIM), lambda b, g, i: (b, g, 0, 0)),
                  pl.BlockSpec((None, None, HEAD_DIM, nchunk), lambda b, g, i: (b, g, 0, 0)),
                  k_spec, v_spec, k_spec, v_spec,
                  pl.BlockSpec((None, N_PAIR_TABLES, KT, GQ), lambda b, g, i: (g, 0, 0, 0)),
                  pl.BlockSpec((None, CMP_NEAR + 8, GQ), lambda b, g, i: (g, 0, 0)),
                  pl.BlockSpec((None, 16, GQ), lambda b, g, i: (g, 0, 0)),
                  pl.BlockSpec((HEAD_DIM, 1), lambda b, g, i: (0, 0)),
                  pl.BlockSpec((n_s, nchunk), lambda b, g, i: (0, 0))],
        out_specs=pl.BlockSpec((Q_BLOCK, GQ), lambda b, g, i: (b * nb + i, g)),
        out_shape=jax.ShapeDtypeStruct((B * T, D_NSA), BF16),
        scratch_shapes=[pltpu.VMEM((HEAD_DIM, GQ), F32),
                        pltpu.VMEM((K_AUG, GQ), BF16),
                        pltpu.VMEM((nchunk, GQ), F32)],
        compiler_params=_cparams(("parallel", "parallel", "arbitrary")),
        name="nsa_attention",
    )(proj, proj, kc, vct, ks, vst, kw, vwt, tp, cbt, brow, q_norm_g.reshape(HEAD_DIM, 1), _overlap_t(T))


def _ret_tables(T):
    H, C = RET_HEADS, RET_CHUNK
    half = RET_HEAD_DIM // 2
    pos = jnp.arange(T, dtype=F32)
    freqs = ROPE_BASE ** (-jnp.arange(half, dtype=F32) / half)
    ang = pos[:, None] * freqs[None, :]
    log_gamma = jnp.log(1.0 - 2.0 ** (-5.0 - jnp.arange(H, dtype=F32)))
    n = jnp.arange(C, dtype=F32)
    diff = n[:, None] - n[None, :]
    decay_in = jnp.where(diff >= 0, jnp.exp(jnp.maximum(diff, 0.0)[None] * log_gamma[:, None, None]), 0.0)
    q_decay = jnp.exp((n + 1.0)[None] * log_gamma[:, None])
    k_decay = jnp.exp((C - 1.0 - n)[None] * log_gamma[:, None])
    chunk_decay = jnp.exp(C * log_gamma)
    qk_decay = jnp.stack([q_decay, k_decay], axis=1)[..., None]
    return jnp.cos(ang), jnp.sin(ang), decay_in, qk_decay, chunk_decay


def _retention_kernel(cd_ref, q_ref, k_ref, v_ref, g_ref, cos_ref, sin_ref, din_ref, qkd_ref, gn_ref,
                      o_ref, state_ref):
    c = pl.program_id(1)
    d = RET_HEAD_DIM
    half = d // 2

    @pl.when(c == 0)
    def _():
        state_ref[...] = jnp.zeros_like(state_ref)

    cos = cos_ref[...]
    sin = sin_ref[...]

    def rope(x):
        x1, x2 = x[:, :half], x[:, half:]
        return jnp.concatenate([x1 * cos - x2 * sin, x2 * cos + x1 * sin], axis=1)

    for h in range(RET_HEADS):
        sl = slice(h * d, (h + 1) * d)
        qr = rope(q_ref[:, sl].astype(F32))
        kr = rope(k_ref[:, sl].astype(F32)) * (d ** -0.5)
        v = v_ref[:, sl]
        qb = qr.astype(BF16)
        inner = lax.dot_general(qb, kr.astype(BF16), (((1,), (1,)), ((), ())), preferred_element_type=F32)
        inner = (inner * din_ref[h]).astype(BF16)
        state = state_ref[h]
        cross = jnp.dot((qr * qkd_ref[h, 0]).astype(BF16), state.astype(BF16), preferred_element_type=F32)
        o = jnp.dot(inner, v, preferred_element_type=F32) + cross
        kdt = (kr * qkd_ref[h, 1]).T.astype(BF16)
        state_ref[h] = state * cd_ref[h] + jnp.dot(kdt, v, preferred_element_type=F32)
        mu = jnp.mean(o, axis=-1, keepdims=True)
        var = jnp.mean(jnp.square(o - mu), axis=-1, keepdims=True)
        on = (o - mu) * lax.rsqrt(var + EPS) * gn_ref[:, sl]
        o_ref[:, sl] = (jax.nn.silu(g_ref[:, sl].astype(F32)) * on).astype(o_ref.dtype)


def retention(proj, gn_g, B, T):
    nc = T // RET_CHUNK
    cos, sin, decay_in, qk_decay, chunk_decay = _ret_tables(T)
    cblk = COL_RET // D_RET

    def seg(k):
        return pl.BlockSpec((RET_CHUNK, D_RET), lambda b, c: (b * nc + c, cblk + k))

    rope_spec = pl.BlockSpec((RET_CHUNK, RET_HEAD_DIM // 2), lambda b, c: (c, 0))
    return pl.pallas_call(
        _retention_kernel,
        grid=(B, nc),
        in_specs=[pl.BlockSpec(memory_space=pltpu.SMEM),
                  seg(0), seg(1), seg(2), seg(3), rope_spec, rope_spec,
                  pl.BlockSpec((RET_HEADS, RET_CHUNK, RET_CHUNK), lambda b, c: (0, 0, 0)),
                  pl.BlockSpec((RET_HEADS, 2, RET_CHUNK, 1), lambda b, c: (0, 0, 0, 0)),
                  pl.BlockSpec((1, D_RET), lambda b, c: (0, 0))],
        out_specs=pl.BlockSpec((RET_CHUNK, D_RET), lambda b, c: (b * nc + c, 0)),
        out_shape=jax.ShapeDtypeStruct((B * T, D_RET), BF16),
        scratch_shapes=[pltpu.VMEM((RET_HEADS, RET_HEAD_DIM, RET_HEAD_DIM), F32)],
        compiler_params=_cparams(("parallel", "arbitrary")),
        name="retention",
    )(chunk_decay, proj, proj, proj, proj, cos, sin, decay_in, qk_decay, gn_g.reshape(1, D_RET))


def kernel(x, norm1_g, w_in, nsa_q_norm_g, nsa_k_norm_g, cmp_pos, cmp_w1, cmp_w2, ret_gn_g,
           w_up_nsa, w_up_ret, w_out, norm2_g, w_ff1, w_ff2, rel_bias):
    B, T, D = x.shape
    depth = w_in.shape[0]
    xf = x.reshape(B * T, D)
    tp, cbt, brow = bias_tables(rel_bias)
    w_in_t = jnp.swapaxes(w_in, 1, 2)
    for l in range(depth):
        h = rmsnorm_bf16(xf, norm1_g[l])
        proj, w_out_b, w_upa_b, w_upr_b = matmul_wcast(
            h, w_in_t, l, n_out=D_PROJ, transposed=True, n_plain=COL_RET // 1024, seg_start=N_REAL_A,
            sides=((w_out, 32), (w_up_nsa, 16), (w_up_ret, 16)), name="in_proj")
        ks, vst, kw, vwt = kv_prep(proj, nsa_k_norm_g[l], B, T)
        kc, vct = compress_kv(proj, cmp_pos[l], cmp_w1[l], cmp_w2[l], nsa_k_norm_g[l], B, T)
        o_a = nsa_attention(proj, nsa_q_norm_g[l], kc, vct, ks, vst, kw, vwt, tp, cbt, brow, B, T)
        o_r = retention(proj, ret_gn_g[l], B, T)
        y = merge_up_proj(o_a, w_upa_b, o_r, w_upr_b, proj)
        xf = matmul_residual(y, w_out_b, xf, name="out_proj")
        h2 = rmsnorm_bf16(xf, norm2_g[l])
        a, w_ff2_b = matmul_wcast(h2, w_ff1, l, n_out=D_FF, relu2=True, sides=((w_ff2, 128),), name="ffn_up")
        xf = matmul_residual(a, w_ff2_b, xf, name="ffn_down")
    return xf.reshape(B, T, D)
```

```python
import functools
import math

import jax
import jax.numpy as jnp
from jax import lax
from jax.experimental import pallas as pl
from jax.experimental.pallas import tpu as pltpu

F32 = jnp.float32
BF16 = jnp.bfloat16

D_MODEL = 4096
HEAD_DIM = 128
D_NSA = D_MODEL // 2
NSA_HEADS = D_NSA // HEAD_DIM
NSA_KV_HEADS = 2
NSA_GROUP = NSA_HEADS // NSA_KV_HEADS
D_KV = NSA_KV_HEADS * HEAD_DIM
CMP_LEN = 32
CMP_STRIDE = 16
CMP_HIDDEN = 256
SEL_LEN = 64
SEL_TOPN = 16
WINDOW = 512
Q_BLOCK = 128
D_RET = D_MODEL // 2
RET_HEADS = 8
RET_HEAD_DIM = D_RET // RET_HEADS
RET_CHUNK = 128
D_FF = 4 * D_MODEL
N_BUCKETS = 32
MAX_DISTANCE = 128
ROPE_BASE = 10000.0
EPS = 1e-6
NEG_INF = -1e30
FORCE_SCORE = 1e9
LOG2E = 1.4426950408889634

COL_Q = 0
COL_KC = D_NSA
COL_KSL = COL_KC + 2 * D_KV
COL_VSL = COL_KSL + D_KV
COL_KWN = COL_VSL + D_KV
COL_VWN = COL_KWN + D_KV
COL_GATE = COL_VWN + D_KV
N_GATE = 3 * NSA_HEADS
COL_RET = 4096
COL_MERGE_A = COL_RET + 4 * D_RET
COL_MERGE_R = COL_MERGE_A + D_MODEL
D_PROJ = COL_MERGE_R + D_MODEL
N_REAL_A = D_NSA + 6 * D_KV + N_GATE

LANES = 128
GQ = NSA_GROUP * Q_BLOCK
KT = 2 * Q_BLOCK
K_AUG = HEAD_DIM + LANES
BIAS_ROW0 = HEAD_DIM
SEL_ROW0 = HEAD_DIM + 16
CMP_PER_QB = Q_BLOCK // CMP_STRIDE
CMP_NEAR = 2 * CMP_PER_QB
VMEM_LIMIT = 60 * 1024 * 1024


def _cparams(sem, vmem=VMEM_LIMIT):
    return pltpu.CompilerParams(dimension_semantics=sem, vmem_limit_bytes=vmem)


def _rmsnorm_kernel(x_ref, g_ref, xg_ref, rs_ref):
    x = x_ref[...]
    xg_ref[...] = (x * g_ref[...]).astype(BF16)
    rs_ref[...] = lax.rsqrt(jnp.mean(x * x, axis=-1, keepdims=True) + EPS)


def rmsnorm_split(x, g, tm=256):
    n, d = x.shape
    return pl.pallas_call(
        _rmsnorm_kernel,
        grid=(n // tm,),
        in_specs=[pl.BlockSpec((tm, d), lambda i: (i, 0)),
                  pl.BlockSpec((1, d), lambda i: (0, 0))],
        out_specs=[pl.BlockSpec((tm, d), lambda i: (i, 0)),
                   pl.BlockSpec((tm, 1), lambda i: (i, 0))],
        out_shape=[jax.ShapeDtypeStruct((n, d), BF16), jax.ShapeDtypeStruct((n, 1), F32)],
        compiler_params=_cparams(("parallel",)),
        name="rmsnorm",
    )(x, g.reshape(1, d))


W_CAST_VREGS = 256


def _mm_ws_kernel(a_ref, w_hbm, rs_ref, *rest, layer, transposed, n_plain, seg_start, relu2, n_side):
    side_refs, rest = rest[:n_side], rest[n_side:]
    o_ref, rest = rest[0], rest[1:]
    side_o_refs, (stage_ref, wb_ref, sem) = rest[:n_side], rest[n_side:]
    for src, dst in zip(side_refs, side_o_refs):
        dst[...] = src[...].astype(BF16)
    j = pl.program_id(0)
    i = pl.program_id(1)
    nj = pl.num_programs(0)
    tn = o_ref.shape[1]

    def w_copy(jj):
        if transposed:
            start = jnp.where(jj < n_plain, jj * tn, seg_start + (jj - n_plain) * tn)
            src = w_hbm.at[layer, pl.ds(pl.multiple_of(start, 16), tn), :]
        else:
            src = w_hbm.at[layer, :, pl.ds(pl.multiple_of(jj * tn, tn), tn)]
        return pltpu.make_async_copy(src, stage_ref, sem)

    @pl.when(i == 0)
    def _():
        @pl.when(j == 0)
        def _():
            w_copy(j).start()

        w_copy(j).wait()
        rows, cols = stage_ref.shape
        step = min(rows, W_CAST_VREGS * 8 * LANES // cols)
        assert rows % step == 0

        def cast(r, _):
            sl = pl.ds(pl.multiple_of(r * step, step), step)
            wb_ref[sl, :] = stage_ref[sl, :].astype(BF16)
            return 0

        lax.fori_loop(0, rows // step, cast, 0)

        @pl.when(j + 1 < nj)
        def _():
            w_copy(j + 1).start()

    if transposed:
        acc = lax.dot_general(a_ref[...], wb_ref[...], (((1,), (1,)), ((), ())), preferred_element_type=F32)
    else:
        acc = jnp.dot(a_ref[...], wb_ref[...], preferred_element_type=F32)
    acc = acc * rs_ref[...]
    if relu2:
        acc = jnp.square(jnp.maximum(acc, 0.0))
    o_ref[...] = acc.astype(o_ref.dtype)


def matmul_wcast(a, row_scale, w3, layer, *, n_out, transposed=False, n_plain=None, seg_start=0, relu2=False,
                 sides=(), tm=1024, tn=1024, name="mm"):
    m, k = a.shape
    tm = min(tm, m)
    nj, ni = n_out // tn, m // tm
    n_plain = nj if n_plain is None else n_plain
    assert transposed or n_plain == nj
    stage_shape = (tn, k) if transposed else (k, tn)
    in_specs = [pl.BlockSpec((tm, k), lambda j, i: (i, 0)),
                pl.BlockSpec(memory_space=pl.ANY),
                pl.BlockSpec((tm, 1), lambda j, i: (i, 0))]
    out_specs = [pl.BlockSpec((tm, tn), lambda j, i: (i, j))]
    out_shape = [jax.ShapeDtypeStruct((m, n_out), BF16)]
    args = [a, w3, row_scale]
    side_in, side_out = [], []
    for arr, slab in sides:
        _, r, c = arr.shape
        n_slab = r // slab
        assert n_slab * slab == r and n_slab <= nj * ni and slab % 16 == 0
        side_in.append(pl.BlockSpec((None, slab, c),
                                    lambda j, i, n_slab=n_slab: (layer, jnp.minimum(j * ni + i, n_slab - 1), 0)))
        side_out.append(pl.BlockSpec((slab, c), lambda j, i, n_slab=n_slab: (jnp.minimum(j * ni + i, n_slab - 1), 0)))
        out_shape.append(jax.ShapeDtypeStruct((r, c), BF16))
        args.append(arr)
    in_specs += side_in
    out_specs += side_out
    outs = pl.pallas_call(
        functools.partial(_mm_ws_kernel, layer=layer, transposed=transposed, n_plain=n_plain,
                          seg_start=seg_start, relu2=relu2, n_side=len(sides)),
        grid=(nj, ni),
        in_specs=in_specs,
        out_specs=out_specs,
        out_shape=out_shape,
        scratch_shapes=[pltpu.VMEM(stage_shape, F32),
                        pltpu.VMEM(stage_shape, BF16),
                        pltpu.SemaphoreType.DMA(())],
        compiler_params=_cparams(("arbitrary", "arbitrary")),
        name=name,
    )(*args)
    return outs[0] if not sides else outs


def _mm_res_kernel(a_ref, w_ref, r_ref, *rest, with_norm):
    if with_norm:
        g_ref, o_ref, xg_ref, rs_ref, ssq_ref = rest
    else:
        (o_ref,) = rest
    j, k = pl.program_id(1), pl.program_id(2)
    nj, nk = pl.num_programs(1), pl.num_programs(2)
    acc = jnp.dot(a_ref[...], w_ref[...], preferred_element_type=F32)

    @pl.when(k == 0)
    def _():
        o_ref[...] = r_ref[...] + acc

    @pl.when(k > 0)
    def _():
        o_ref[...] += acc

    if with_norm:
        @pl.when(k == nk - 1)
        def _():
            x = o_ref[...]
            xg_ref[...] = (x * g_ref[...]).astype(BF16)
            part = jnp.sum(x * x, axis=-1, keepdims=True)

            @pl.when(j == 0)
            def _():
                ssq_ref[...] = part

            @pl.when(j > 0)
            def _():
                ssq_ref[...] += part

            @pl.when(j == nj - 1)
            def _():
                rs_ref[...] = lax.rsqrt(ssq_ref[...] * (1.0 / (nj * o_ref.shape[1])) + EPS)


def matmul_residual(a, w, res, *, norm_g=None, tm=1024, tn=1024, tk=2048, name="mm_res"):
    m, k = a.shape
    n = w.shape[-1]
    tm, tn, tk = min(tm, m), min(tn, n), min(tk, k)
    with_norm = norm_g is not None
    tile = pl.BlockSpec((tm, tn), lambda i, j, kk: (i, j))
    in_specs = [pl.BlockSpec((tm, tk), lambda i, j, kk: (i, kk)),
                pl.BlockSpec((tk, tn), lambda i, j, kk: (kk, j)),
                tile]
    out_specs, out_shape, scratch, args = [tile], [jax.ShapeDtypeStruct((m, n), F32)], [], [a, w, res]
    if with_norm:
        in_specs.append(pl.BlockSpec((1, tn), lambda i, j, kk: (0, j)))
        out_specs += [tile, pl.BlockSpec((tm, 1), lambda i, j, kk: (i, 0))]
        out_shape += [jax.ShapeDtypeStruct((m, n), BF16), jax.ShapeDtypeStruct((m, 1), F32)]
        scratch.append(pltpu.VMEM((tm, 1), F32))
        args.append(norm_g.reshape(1, n))
    outs = pl.pallas_call(
        functools.partial(_mm_res_kernel, with_norm=with_norm),
        grid=(m // tm, n // tn, k // tk),
        in_specs=in_specs,
        out_specs=out_specs,
        out_shape=out_shape,
        scratch_shapes=scratch,
        compiler_params=_cparams(("parallel", "arbitrary", "arbitrary")),
        name=name,
    )(*args)
    return outs if with_norm else outs[0]


def _mm_merge_kernel(a1_ref, w1_ref, a2_ref, w2_ref, g1_ref, g2_ref, o_ref):
    u1 = jnp.dot(a1_ref[...], w1_ref[...], preferred_element_type=F32)
    u2 = jnp.dot(a2_ref[...], w2_ref[...], preferred_element_type=F32)
    y = (jax.nn.sigmoid(g1_ref[...].astype(F32)) * u1
         + jax.nn.sigmoid(g2_ref[...].astype(F32)) * u2)
    o_ref[...] = y.astype(o_ref.dtype)


def merge_up_proj(o_a, w_a, o_r, w_r, proj, *, tm=1024, tn=1024):
    m, k = o_a.shape
    n = w_a.shape[1]
    tm = min(tm, m)
    ca, cr = COL_MERGE_A // tn, COL_MERGE_R // tn
    return pl.pallas_call(
        _mm_merge_kernel,
        grid=(m // tm, n // tn),
        in_specs=[pl.BlockSpec((tm, k), lambda i, j: (i, 0)),
                  pl.BlockSpec((k, tn), lambda i, j: (0, j)),
                  pl.BlockSpec((tm, k), lambda i, j: (i, 0)),
                  pl.BlockSpec((k, tn), lambda i, j: (0, j)),
                  pl.BlockSpec((tm, tn), lambda i, j: (i, ca + j)),
                  pl.BlockSpec((tm, tn), lambda i, j: (i, cr + j))],
        out_specs=pl.BlockSpec((tm, tn), lambda i, j: (i, j)),
        out_shape=jax.ShapeDtypeStruct((m, n), BF16),
        compiler_params=_cparams(("parallel", "arbitrary")),
        name="merge_up",
    )(o_a, w_a, o_r, w_r, proj, proj)


def _t5_bucket(dist):
    n = jnp.maximum(dist, 0)
    max_exact = N_BUCKETS // 2
    large = max_exact + (jnp.log(jnp.maximum(n, 1).astype(F32) / max_exact)
                         / math.log(MAX_DISTANCE / max_exact) * (N_BUCKETS - max_exact)).astype(jnp.int32)
    large = jnp.minimum(large, N_BUCKETS - 1)
    return jnp.where(n < max_exact, n, large)


N_PAIR_TABLES = 6
ZERO_PAIR_TABLE = 4


def _bucket_maps():
    kk = jnp.arange(Q_BLOCK)[:, None]
    r = jnp.arange(Q_BLOCK)[None, :]

    def tile(delta):
        dist = delta * Q_BLOCK + r - kk
        return jnp.where((dist >= 0) & (dist < WINDOW), _t5_bucket(dist), -1)

    neg = jnp.full((Q_BLOCK, Q_BLOCK), -1)
    far = jnp.full((Q_BLOCK, Q_BLOCK), N_BUCKETS - 1)
    pairs = [(tile(0), neg), (far, tile(1)), (tile(4), far),
             (tile(1), tile(0)), (far, far), (neg, tile(4))]
    pair_map = jnp.concatenate([jnp.concatenate(p, axis=0) for p in pairs], axis=0).astype(jnp.int32)
    mrow = jnp.arange(CMP_NEAR)[:, None]
    dist_c = r - CMP_STRIDE * (mrow - CMP_PER_QB) - (CMP_LEN - 1)
    cmp_rows = jnp.where(dist_c >= 0, _t5_bucket(dist_c), -1)
    cmp_map = jnp.concatenate([cmp_rows, jnp.full((8, Q_BLOCK), N_BUCKETS - 1)], axis=0).astype(jnp.int32)
    return pair_map, cmp_map


def _bias_kernel(relb_ref, pmap_ref, cmap_ref, tp_ref, cb_ref, br_ref):
    h = pl.program_id(0)
    far = relb_ref[N_BUCKETS - 1, h]

    def lookup(bmap, offset):
        out = jnp.full(bmap.shape, NEG_INF, F32)
        for b in range(N_BUCKETS):
            out = jnp.where(bmap == b, (relb_ref[b, h] - offset) * LOG2E, out)
        return out

    tp_ref[...] = lookup(pmap_ref[...], far)
    cb_ref[...] = lookup(cmap_ref[...], 0.0)
    full = jnp.full(br_ref.shape, far * LOG2E, F32)
    hi = full.astype(BF16).astype(F32)
    row = lax.broadcasted_iota(jnp.int32, br_ref.shape, 0)
    br_ref[...] = jnp.where(row == 0, hi, jnp.where(row == 1, full - hi, 0.0))


def bias_tables(rel_bias):
    pair_map, cmp_map = _bucket_maps()
    rp, rc = pair_map.shape[0], cmp_map.shape[0]

    def out_spec(rows):
        return pl.BlockSpec((None, rows, LANES), lambda h: (h // NSA_GROUP, 0, h % NSA_GROUP))

    tp, cb, br = pl.pallas_call(
        _bias_kernel,
        grid=(NSA_HEADS,),
        in_specs=[pl.BlockSpec(memory_space=pltpu.SMEM),
                  pl.BlockSpec((rp, LANES), lambda h: (0, 0)),
                  pl.BlockSpec((rc, LANES), lambda h: (0, 0))],
        out_specs=[out_spec(rp), out_spec(rc), out_spec(16)],
        out_shape=[jax.ShapeDtypeStruct((NSA_KV_HEADS, rp, GQ), F32),
                   jax.ShapeDtypeStruct((NSA_KV_HEADS, rc, GQ), F32),
                   jax.ShapeDtypeStruct((NSA_KV_HEADS, 16, GQ), F32)],
        compiler_params=_cparams(("arbitrary",)),
        name="bias_tables",
    )(rel_bias, pair_map, cmp_map)
    return tp.reshape(NSA_KV_HEADS, N_PAIR_TABLES, KT, GQ), cb, br


def _kv_prep_kernel(ks_ref, vs_ref, kw_ref, vw_ref, g_ref, kso_ref, vso_ref, kwo_ref, vwo_ref):
    p = pl.program_id(2)

    def norm(x, g):
        ms = jnp.mean(x * x, axis=-1, keepdims=True)
        return x * lax.rsqrt(ms + EPS) * g

    lane = lax.broadcasted_iota(jnp.int32, (KT, LANES), 1)
    row = lax.broadcasted_iota(jnp.int32, (KT, LANES), 0)
    ones = jnp.where(lane < 2, 1.0, 0.0)
    blk = (KT // SEL_LEN) * p + jnp.right_shift(row, 6)
    onehot = jnp.where(lane == blk + (SEL_ROW0 - HEAD_DIM), 1.0, 0.0)
    ks = norm(ks_ref[...].astype(F32), g_ref[1:2, :])
    kw = norm(kw_ref[...].astype(F32), g_ref[2:3, :])
    kso_ref[...] = jnp.concatenate([ks, ones + onehot], axis=1).astype(BF16)
    kwo_ref[...] = jnp.concatenate([kw, ones], axis=1).astype(BF16)
    vso_ref[...] = vs_ref[...].astype(F32).T.astype(BF16)
    vwo_ref[...] = vw_ref[...].astype(F32).T.astype(BF16)


def kv_prep(proj, k_norm_g, B, T):
    npair = T // KT
    G = NSA_KV_HEADS
    assert SEL_LEN == 64 and SEL_ROW0 + T // SEL_LEN <= K_AUG

    def col(c0):
        return pl.BlockSpec((KT, HEAD_DIM), lambda b, g, p: (b * npair + p, c0 // HEAD_DIM + g))

    k_spec = pl.BlockSpec((None, None, KT, K_AUG), lambda b, g, p: (b, g, p, 0))
    v_spec = pl.BlockSpec((None, None, None, HEAD_DIM, KT), lambda b, g, p: (b, g, p, 0, 0))
    k_shape = jax.ShapeDtypeStruct((B, G, T, K_AUG), BF16)
    v_shape = jax.ShapeDtypeStruct((B, G, npair, HEAD_DIM, KT), BF16)
    return pl.pallas_call(
        _kv_prep_kernel,
        grid=(B, G, npair),
        in_specs=[col(COL_KSL), col(COL_VSL), col(COL_KWN), col(COL_VWN),
                  pl.BlockSpec((3, HEAD_DIM), lambda b, g, p: (0, 0))],
        out_specs=[k_spec, v_spec, k_spec, v_spec],
        out_shape=[k_shape, v_shape, k_shape, v_shape],
        compiler_params=_cparams(("parallel", "parallel", "parallel")),
        name="kv_prep",
    )(proj, proj, proj, proj, k_norm_g)


def _compress_kernel(ck_ref, cv_ref, pos_ref, w1_ref, w2_ref, g_ref, kc_ref, vct_ref):
    nchunk = ck_ref.shape[0]
    half = CMP_STRIDE * HEAD_DIM

    def mlp(x_ref, idx):
        x = x_ref[...].astype(F32)
        xa = (x + pos_ref[idx, :, 0:half]).astype(BF16)
        xb = (x + pos_ref[idx, :, half:2 * half]).astype(BF16)
        u = jnp.dot(xa, w1_ref[idx, 0:half, :], preferred_element_type=F32)
        v = jnp.dot(xb, w1_ref[idx, half:2 * half, :], preferred_element_type=F32)
        pre = u + pltpu.roll(v, nchunk - 1, 0)
        return jnp.dot(jax.nn.gelu(pre).astype(BF16), w2_ref[idx], preferred_element_type=F32)

    kc = mlp(ck_ref, 0)
    ms = jnp.mean(kc * kc, axis=-1, keepdims=True)
    kc_ref[...] = (kc * lax.rsqrt(ms + EPS) * g_ref[0:1, :]).astype(BF16)
    vct_ref[...] = mlp(cv_ref, 1).T.astype(BF16)


def compress_kv(proj, cmp_pos, cmp_w1, cmp_w2, k_norm_g, B, T):
    G = NSA_KV_HEADS
    nchunk = T // CMP_STRIDE
    half = CMP_STRIDE * HEAD_DIM
    ckv = proj[:, COL_KC:COL_KC + 2 * D_KV].reshape(B, nchunk, CMP_STRIDE, 2, G, HEAD_DIM)
    ckv = ckv.transpose(3, 0, 4, 1, 2, 5).reshape(2, B, G, nchunk, half)
    pos = cmp_pos.reshape(2, 1, CMP_LEN * HEAD_DIM)
    in_spec = pl.BlockSpec((None, None, nchunk, half), lambda b, g: (b, g, 0, 0))
    return pl.pallas_call(
        _compress_kernel,
        grid=(B, G),
        in_specs=[in_spec, in_spec,
                  pl.BlockSpec((2, 1, CMP_LEN * HEAD_DIM), lambda b, g: (0, 0, 0)),
                  pl.BlockSpec((2, CMP_LEN * HEAD_DIM, CMP_HIDDEN), lambda b, g: (0, 0, 0)),
                  pl.BlockSpec((2, CMP_HIDDEN, HEAD_DIM), lambda b, g: (0, 0, 0)),
                  pl.BlockSpec((3, HEAD_DIM), lambda b, g: (0, 0))],
        out_specs=[pl.BlockSpec((None, None, nchunk, HEAD_DIM), lambda b, g: (b, g, 0, 0)),
                   pl.BlockSpec((None, None, HEAD_DIM, nchunk), lambda b, g: (b, g, 0, 0))],
        out_shape=[jax.ShapeDtypeStruct((B, G, nchunk, HEAD_DIM), BF16),
                   jax.ShapeDtypeStruct((B, G, HEAD_DIM, nchunk), BF16)],
        compiler_params=_cparams(("parallel", "parallel")),
        name="compress_kv",
    )(ckv[0], ckv[1], pos, cmp_w1.astype(BF16), cmp_w2.astype(BF16), k_norm_g)


def _nsa_kernel(q_ref, gate_ref, kc_ref, vct_ref, ks_ref, vst_ref, kw_ref, vwt_ref,
                tp_ref, cbt_ref, brow_ref, qg_ref, ovl_ref, o_ref, acc_ref, qa_ref, cb_ref, sa_ref, sb_ref,
                *, top_n):
    i = pl.program_id(2)
    parity = jnp.bitwise_and(i, 1)
    pd = jnp.right_shift(i, 1)
    nchunk = kc_ref.shape[0]
    n_s = ovl_ref.shape[0]
    scale = HEAD_DIM ** -0.5 * LOG2E

    q = q_ref[...].astype(F32)
    qt = jnp.concatenate([q[:, h * HEAD_DIM:(h + 1) * HEAD_DIM].T for h in range(NSA_GROUP)], axis=1)
    ms = jnp.mean(qt * qt, axis=0, keepdims=True)
    q2t = (qt * lax.rsqrt(ms + EPS) * (qg_ref[...] * scale)).astype(BF16)
    qa_ref[0:HEAD_DIM, :] = q2t
    qa_ref[BIAS_ROW0:SEL_ROW0, :] = brow_ref[...].astype(BF16)
    qa_ref[SEL_ROW0:K_AUG, :] = jnp.zeros((K_AUG - SEL_ROW0, GQ), BF16)

    crow = lax.broadcasted_iota(jnp.int32, (nchunk, GQ), 0)
    near0 = CMP_PER_QB * i - CMP_PER_QB
    cb_ref[...] = jnp.where(crow < near0, cbt_ref[CMP_NEAR:CMP_NEAR + 1, :], NEG_INF)

    @pl.when(i == 0)
    def _():
        cb_ref[0:CMP_PER_QB, :] = cbt_ref[CMP_PER_QB:CMP_NEAR, :]

    @pl.when(i > 0)
    def _():
        cb_ref[pl.ds(pl.multiple_of(near0, CMP_PER_QB), CMP_NEAR), :] = cbt_ref[0:CMP_NEAR, :]

    lc = jnp.dot(kc_ref[...], q2t, preferred_element_type=F32) + cb_ref[...]
    mc = jnp.max(lc, axis=0, keepdims=True)
    ec = jnp.where(lc > 0.5 * NEG_INF, jnp.exp2(lc - mc), 0.0)
    pc = ec * (1.0 / jnp.maximum(jnp.sum(ec, axis=0, keepdims=True), 1e-30))
    o_cmp = jnp.dot(vct_ref[...], pc.astype(BF16), preferred_element_type=F32)

    pt = pc[:, 0:Q_BLOCK]
    for h in range(1, NSA_GROUP):
        pt = pt + pc[:, h * Q_BLOCK:(h + 1) * Q_BLOCK]
    ovl = ovl_ref[...]
    p1 = pt.astype(BF16)
    r1 = pt - p1.astype(F32)
    p2 = r1.astype(BF16)
    p3 = (r1 - p2.astype(F32)).astype(BF16)
    imp = (jnp.dot(ovl, p1, preferred_element_type=F32)
           + jnp.dot(ovl, p2, preferred_element_type=F32)
           + jnp.dot(ovl, p3, preferred_element_type=F32))
    sidx = lax.broadcasted_iota(jnp.int32, (n_s, Q_BLOCK), 0)
    rq = lax.broadcasted_iota(jnp.int32, (n_s, Q_BLOCK), 1)
    cur = (Q_BLOCK // SEL_LEN) * i + jnp.where(rq >= SEL_LEN, 1, 0)
    forced = (sidx == 0) | (sidx == cur) | (sidx == cur - 1)
    score = jnp.where(forced, FORCE_SCORE, jnp.where(sidx > cur, NEG_INF, imp))
    rank = jnp.zeros((n_s, Q_BLOCK), F32)
    for sp in range(n_s):
        row = score[sp:sp + 1, :]
        rank = rank + jnp.where(sidx > sp, jnp.where(row >= score, 1.0, 0.0), jnp.where(row > score, 1.0, 0.0))
    selb = jnp.where(rank < top_n, 0.0, NEG_INF)
    qa_ref[SEL_ROW0:SEL_ROW0 + n_s, :] = jnp.concatenate([selb] * NSA_GROUP, axis=1).astype(BF16)

    def flash(k_ref, vt_ref, count, pair_of, table_of):
        last = count - 1

        def logits(t):
            p = pair_of(jnp.minimum(t, last))
            k = k_ref[pl.ds(pl.multiple_of(p * KT, KT), KT), :]
            return jnp.dot(k, qa_ref[...], preferred_element_type=F32) + tp_ref[table_of(p)]

        def consume(s_ref, t, carry):
            m, l = carry
            s = s_ref[...]
            m_new = jnp.maximum(m, jnp.max(s, axis=0, keepdims=True))
            alpha = jnp.exp2(m - m_new)
            pe = jnp.exp2(s - m_new)
            l = alpha * l + jnp.sum(pe, axis=0, keepdims=True)
            acc_ref[...] = acc_ref[...] * alpha + jnp.dot(vt_ref[pair_of(t)], pe.astype(BF16),
                                                          preferred_element_type=F32)
            return m_new, l

        def two_pairs(u, carry):
            t = 2 * u
            carry = consume(sa_ref, t, carry)
            sb_ref[...] = logits(t + 1)
            carry = consume(sb_ref, t + 1, carry)
            sa_ref[...] = logits(t + 2)
            return carry

        acc_ref[...] = jnp.zeros_like(acc_ref)
        sa_ref[...] = logits(0)
        carry = (jnp.full((1, GQ), NEG_INF, F32), jnp.zeros((1, GQ), F32))
        carry = lax.fori_loop(0, jnp.right_shift(count, 1), two_pairs, carry)
        _, l = lax.cond(jnp.bitwise_and(count, 1) == 1, lambda c: consume(sa_ref, last, c), lambda c: c, carry)
        return acc_ref[...] * (1.0 / l)

    def near_table(p):
        return 3 * parity + (pd - p)

    o_sel = flash(ks_ref, vst_ref, pd + 1, lambda t: t,
                  lambda p: jnp.where(pd - p <= 1, near_table(p), ZERO_PAIR_TABLE))

    o_win = flash(kw_ref, vwt_ref, jnp.minimum(pd, 2) + 1, lambda t: pd - t, near_table)

    gt = jax.nn.sigmoid(gate_ref[...].astype(F32)).T
    first_group = pl.program_id(1) == 0

    def gate_row(br):
        rows = []
        for h in range(NSA_GROUP):
            lo = 3 * h + br
            hi = 3 * (NSA_GROUP + h) + br
            rows.append(jnp.where(first_group, gt[lo:lo + 1, :], gt[hi:hi + 1, :]))
        return jnp.concatenate(rows, axis=1)

    ot = gate_row(0) * o_cmp + gate_row(1) * o_sel + gate_row(2) * o_win
    for h in range(NSA_GROUP):
        o_ref[:, h * HEAD_DIM:(h + 1) * HEAD_DIM] = ot[:, h * Q_BLOCK:(h + 1) * Q_BLOCK].T.astype(o_ref.dtype)


def _overlap_t(T):
    n_c = T // CMP_STRIDE
    n_s = T // SEL_LEN
    c_start = jnp.arange(n_c) * CMP_STRIDE
    c_end = c_start + CMP_LEN - 1
    s_start = jnp.arange(n_s) * SEL_LEN
    ov = (c_start[None, :] < s_start[:, None] + SEL_LEN) & (c_end[None, :] >= s_start[:, None])
    ov = ov & (jnp.arange(n_c)[None, :] < n_c - 1)
    return ov.astype(BF16)


def nsa_attention(proj, q_norm_g, kc, vct, ks, vst, kw, vwt, tp, cbt, brow, B, T):
    nb = T // Q_BLOCK
    npair = T // KT
    G = NSA_KV_HEADS
    nchunk = T // CMP_STRIDE
    n_s = T // SEL_LEN
    top_n = min(SEL_TOPN, n_s)
    k_spec = pl.BlockSpec((None, None, T, K_AUG), lambda b, g, i: (b, g, 0, 0))
    v_spec = pl.BlockSpec((None, None, npair, HEAD_DIM, KT), lambda b, g, i: (b, g, 0, 0, 0))
    gate_blk = COL_GATE // LANES
    return pl.pallas_call(
        functools.partial(_nsa_kernel, top_n=top_n),
        grid=(B, G, nb),
        in_specs=[pl.BlockSpec((Q_BLOCK, GQ), lambda b, g, i: (b * nb + i, g)),
                  pl.BlockSpec((Q_BLOCK, LANES), lambda b, g, i: (b * nb + i, gate_blk)),
                  pl.BlockSpec((None, None, nchunk, HEAD_DIM), lambda b, g, i: (b, g, 0, 0)),
                  pl.BlockSpec((None, None, HEAD_DIM, nchunk), lambda b, g, i: (b, g, 0, 0)),
                  k_spec, v_spec, k_spec, v_spec,
                  pl.BlockSpec((None, N_PAIR_TABLES, KT, GQ), lambda b, g, i: (g, 0, 0, 0)),
                  pl.BlockSpec((None, CMP_NEAR + 8, GQ), lambda b, g, i: (g, 0, 0)),
                  pl.BlockSpec((None, 16, GQ), lambda b, g, i: (g, 0, 0)),
                  pl.BlockSpec((HEAD_DIM, 1), lambda b, g, i: (0, 0)),
                  pl.BlockSpec((n_s, nchunk), lambda b, g, i: (0, 0))],
        out_specs=pl.BlockSpec((Q_BLOCK, GQ), lambda b, g, i: (b * nb + i, g)),
        out_shape=jax.ShapeDtypeStruct((B * T, D_NSA), BF16),
        scratch_shapes=[pltpu.VMEM((HEAD_DIM, GQ), F32),
                        pltpu.VMEM((K_AUG, GQ), BF16),
                        pltpu.VMEM((nchunk, GQ), F32),
                        pltpu.VMEM((KT, GQ), F32),
                        pltpu.VMEM((KT, GQ), F32)],
        compiler_params=_cparams(("parallel", "parallel", "arbitrary")),
        name="nsa_attention",
    )(proj, proj, kc, vct, ks, vst, kw, vwt, tp, cbt, brow, q_norm_g.reshape(HEAD_DIM, 1), _overlap_t(T))


def _ret_tables(T):
    H, C = RET_HEADS, RET_CHUNK
    half = RET_HEAD_DIM // 2
    pos = jnp.arange(T, dtype=F32)
    freqs = ROPE_BASE ** (-jnp.arange(half, dtype=F32) / half)
    ang = pos[:, None] * freqs[None, :]
    log_gamma = jnp.log(1.0 - 2.0 ** (-5.0 - jnp.arange(H, dtype=F32)))
    n = jnp.arange(C, dtype=F32)
    diff = n[:, None] - n[None, :]
    decay_in = jnp.where(diff >= 0, jnp.exp(jnp.maximum(diff, 0.0)[None] * log_gamma[:, None, None]), 0.0)
    q_decay = jnp.exp((n + 1.0)[None] * log_gamma[:, None])
    k_decay = jnp.exp((C - 1.0 - n)[None] * log_gamma[:, None])
    chunk_decay = jnp.exp(C * log_gamma)
    qk_decay = jnp.stack([q_decay, k_decay], axis=1)[..., None]
    return jnp.cos(ang), jnp.sin(ang), decay_in, qk_decay, chunk_decay


def _retention_kernel(cd_ref, q_ref, k_ref, v_ref, g_ref, cos_ref, sin_ref, din_ref, qkd_ref, gn_ref,
                      o_ref, state_ref):
    c = pl.program_id(1)
    d = RET_HEAD_DIM
    half = d // 2

    @pl.when(c == 0)
    def _():
        state_ref[...] = jnp.zeros_like(state_ref)

    cos = cos_ref[...]
    sin = sin_ref[...]

    def rope(x):
        x1, x2 = x[:, :half], x[:, half:]
        return jnp.concatenate([x1 * cos - x2 * sin, x2 * cos + x1 * sin], axis=1)

    for h in range(RET_HEADS):
        sl = slice(h * d, (h + 1) * d)
        qr = rope(q_ref[:, sl].astype(F32))
        kr = rope(k_ref[:, sl].astype(F32)) * (d ** -0.5)
        v = v_ref[:, sl]
        qb = qr.astype(BF16)
        inner = lax.dot_general(qb, kr.astype(BF16), (((1,), (1,)), ((), ())), preferred_element_type=F32)
        inner = (inner * din_ref[h]).astype(BF16)
        state = state_ref[h]
        cross = jnp.dot((qr * qkd_ref[h, 0]).astype(BF16), state.astype(BF16), preferred_element_type=F32)
        o = jnp.dot(inner, v, preferred_element_type=F32) + cross
        kdt = (kr * qkd_ref[h, 1]).T.astype(BF16)
        state_ref[h] = state * cd_ref[h] + jnp.dot(kdt, v, preferred_element_type=F32)
        mu = jnp.mean(o, axis=-1, keepdims=True)
        var = jnp.mean(jnp.square(o - mu), axis=-1, keepdims=True)
        on = (o - mu) * lax.rsqrt(var + EPS) * gn_ref[:, sl]
        o_ref[:, sl] = (jax.nn.silu(g_ref[:, sl].astype(F32)) * on).astype(o_ref.dtype)


def retention(proj, gn_g, B, T):
    nc = T // RET_CHUNK
    cos, sin, decay_in, qk_decay, chunk_decay = _ret_tables(T)
    cblk = COL_RET // D_RET

    def seg(k):
        return pl.BlockSpec((RET_CHUNK, D_RET), lambda b, c: (b * nc + c, cblk + k))

    rope_spec = pl.BlockSpec((RET_CHUNK, RET_HEAD_DIM // 2), lambda b, c: (c, 0))
    return pl.pallas_call(
        _retention_kernel,
        grid=(B, nc),
        in_specs=[pl.BlockSpec(memory_space=pltpu.SMEM),
                  seg(0), seg(1), seg(2), seg(3), rope_spec, rope_spec,
                  pl.BlockSpec((RET_HEADS, RET_CHUNK, RET_CHUNK), lambda b, c: (0, 0, 0)),
                  pl.BlockSpec((RET_HEADS, 2, RET_CHUNK, 1), lambda b, c: (0, 0, 0, 0)),
                  pl.BlockSpec((1, D_RET), lambda b, c: (0, 0))],
        out_specs=pl.BlockSpec((RET_CHUNK, D_RET), lambda b, c: (b * nc + c, 0)),
        out_shape=jax.ShapeDtypeStruct((B * T, D_RET), BF16),
        scratch_shapes=[pltpu.VMEM((RET_HEADS, RET_HEAD_DIM, RET_HEAD_DIM), F32)],
        compiler_params=_cparams(("parallel", "arbitrary")),
        name="retention",
    )(chunk_decay, proj, proj, proj, proj, cos, sin, decay_in, qk_decay, gn_g.reshape(1, D_RET))


def kernel(x, norm1_g, w_in, nsa_q_norm_g, nsa_k_norm_g, cmp_pos, cmp_w1, cmp_w2, ret_gn_g,
           w_up_nsa, w_up_ret, w_out, norm2_g, w_ff1, w_ff2, rel_bias):
    B, T, D = x.shape
    depth = w_in.shape[0]
    xf = x.reshape(B * T, D)
    tp, cbt, brow = bias_tables(rel_bias)
    w_in_t = jnp.swapaxes(w_in, 1, 2)
    h, rs = rmsnorm_split(xf, norm1_g[0])
    for l in range(depth):
        proj, w_out_b, w_upa_b, w_upr_b = matmul_wcast(
            h, rs, w_in_t, l, n_out=D_PROJ, transposed=True, n_plain=COL_RET // 1024, seg_start=N_REAL_A,
            sides=((w_out, 32), (w_up_nsa, 16), (w_up_ret, 16)), name="in_proj")
        ks, vst, kw, vwt = kv_prep(proj, nsa_k_norm_g[l], B, T)
        kc, vct = compress_kv(proj, cmp_pos[l], cmp_w1[l], cmp_w2[l], nsa_k_norm_g[l], B, T)
        o_a = nsa_attention(proj, nsa_q_norm_g[l], kc, vct, ks, vst, kw, vwt, tp, cbt, brow, B, T)
        o_r = retention(proj, ret_gn_g[l], B, T)
        y = merge_up_proj(o_a, w_upa_b, o_r, w_upr_b, proj)
        xf, h2, rs2 = matmul_residual(y, w_out_b, xf, norm_g=norm2_g[l], name="out_proj")
        a, w_ff2_b = matmul_wcast(h2, rs2, w_ff1, l, n_out=D_FF, relu2=True, sides=((w_ff2, 128),), name="ffn_up")
        if l + 1 < depth:
            xf, h, rs = matmul_residual(a, w_ff2_b, xf, norm_g=norm1_g[l + 1], name="ffn_down")
        else:
            xf = matmul_residual(a, w_ff2_b, xf, name="ffn_down")
    return xf.reshape(B, T, D)
```

```python
import functools
import math

import jax
import jax.numpy as jnp
from jax import lax
from jax.experimental import pallas as pl
from jax.experimental.pallas import tpu as pltpu

F32 = jnp.float32
BF16 = jnp.bfloat16

D_MODEL = 4096
HEAD_DIM = 128
D_NSA = D_MODEL // 2
NSA_HEADS = D_NSA // HEAD_DIM
NSA_KV_HEADS = 2
NSA_GROUP = NSA_HEADS // NSA_KV_HEADS
D_KV = NSA_KV_HEADS * HEAD_DIM
CMP_LEN = 32
CMP_STRIDE = 16
CMP_HIDDEN = 256
SEL_LEN = 64
SEL_TOPN = 16
WINDOW = 512
Q_BLOCK = 128
D_RET = D_MODEL // 2
RET_HEADS = 8
RET_HEAD_DIM = D_RET // RET_HEADS
RET_CHUNK = 128
D_FF = 4 * D_MODEL
N_BUCKETS = 32
MAX_DISTANCE = 128
ROPE_BASE = 10000.0
EPS = 1e-6
NEG_INF = -1e30
FORCE_SCORE = 1e9
LOG2E = 1.4426950408889634

COL_Q = 0
COL_KC = D_NSA
COL_KSL = COL_KC + 2 * D_KV
COL_VSL = COL_KSL + D_KV
COL_KWN = COL_VSL + D_KV
COL_VWN = COL_KWN + D_KV
COL_GATE = COL_VWN + D_KV
N_GATE = 3 * NSA_HEADS
COL_RET = 4096
COL_MERGE_A = COL_RET + 4 * D_RET
COL_MERGE_R = COL_MERGE_A + D_MODEL
D_PROJ = COL_MERGE_R + D_MODEL
N_REAL_A = D_NSA + 6 * D_KV + N_GATE

LANES = 128
GQ = NSA_GROUP * Q_BLOCK
KT = 2 * Q_BLOCK
K_AUG = HEAD_DIM + LANES
BIAS_ROW0 = HEAD_DIM
SEL_ROW0 = HEAD_DIM + 16
CMP_PER_QB = Q_BLOCK // CMP_STRIDE
CMP_NEAR = 2 * CMP_PER_QB
VMEM_LIMIT = 60 * 1024 * 1024


def _cparams(sem, vmem=VMEM_LIMIT):
    return pltpu.CompilerParams(dimension_semantics=sem, vmem_limit_bytes=vmem)


def _rmsnorm_kernel(x_ref, g_ref, xg_ref, rs_ref):
    x = x_ref[...]
    xg_ref[...] = (x * g_ref[...]).astype(BF16)
    rs_ref[...] = lax.rsqrt(jnp.mean(x * x, axis=-1, keepdims=True) + EPS)


def rmsnorm_split(x, g, tm=256):
    n, d = x.shape
    return pl.pallas_call(
        _rmsnorm_kernel,
        grid=(n // tm,),
        in_specs=[pl.BlockSpec((tm, d), lambda i: (i, 0)),
                  pl.BlockSpec((1, d), lambda i: (0, 0))],
        out_specs=[pl.BlockSpec((tm, d), lambda i: (i, 0)),
                   pl.BlockSpec((tm, 1), lambda i: (i, 0))],
        out_shape=[jax.ShapeDtypeStruct((n, d), BF16), jax.ShapeDtypeStruct((n, 1), F32)],
        compiler_params=_cparams(("parallel",)),
        name="rmsnorm",
    )(x, g.reshape(1, d))


W_CAST_VREGS = 256


def _mm_ws_kernel(a_ref, w_hbm, rs_ref, *rest, layer, transposed, n_plain, seg_start, relu2, n_side):
    side_refs, rest = rest[:n_side], rest[n_side:]
    o_ref, rest = rest[0], rest[1:]
    side_o_refs, (stage_ref, wb_ref, sem) = rest[:n_side], rest[n_side:]
    for src, dst in zip(side_refs, side_o_refs):
        dst[...] = src[...].astype(BF16)
    j = pl.program_id(0)
    i = pl.program_id(1)
    nj = pl.num_programs(0)
    tn = o_ref.shape[1]

    def w_copy(jj):
        if transposed:
            start = jnp.where(jj < n_plain, jj * tn, seg_start + (jj - n_plain) * tn)
            src = w_hbm.at[layer, pl.ds(pl.multiple_of(start, 16), tn), :]
        else:
            src = w_hbm.at[layer, :, pl.ds(pl.multiple_of(jj * tn, tn), tn)]
        return pltpu.make_async_copy(src, stage_ref, sem)

    @pl.when(i == 0)
    def _():
        @pl.when(j == 0)
        def _():
            w_copy(j).start()

        w_copy(j).wait()
        rows, cols = stage_ref.shape
        step = min(rows, W_CAST_VREGS * 8 * LANES // cols)
        assert rows % step == 0

        def cast(r, _):
            sl = pl.ds(pl.multiple_of(r * step, step), step)
            wb_ref[sl, :] = stage_ref[sl, :].astype(BF16)
            return 0

        lax.fori_loop(0, rows // step, cast, 0)

        @pl.when(j + 1 < nj)
        def _():
            w_copy(j + 1).start()

    if transposed:
        acc = lax.dot_general(a_ref[...], wb_ref[...], (((1,), (1,)), ((), ())), preferred_element_type=F32)
    else:
        acc = jnp.dot(a_ref[...], wb_ref[...], preferred_element_type=F32)
    acc = acc * rs_ref[...]
    if relu2:
        acc = jnp.square(jnp.maximum(acc, 0.0))
    o_ref[...] = acc.astype(o_ref.dtype)


def matmul_wcast(a, row_scale, w3, layer, *, n_out, transposed=False, n_plain=None, seg_start=0, relu2=False,
                 sides=(), tm=1024, tn=1024, name="mm"):
    m, k = a.shape
    tm = min(tm, m)
    nj, ni = n_out // tn, m // tm
    n_plain = nj if n_plain is None else n_plain
    assert transposed or n_plain == nj
    stage_shape = (tn, k) if transposed else (k, tn)
    in_specs = [pl.BlockSpec((tm, k), lambda j, i: (i, 0)),
                pl.BlockSpec(memory_space=pl.ANY),
                pl.BlockSpec((tm, 1), lambda j, i: (i, 0))]
    out_specs = [pl.BlockSpec((tm, tn), lambda j, i: (i, j))]
    out_shape = [jax.ShapeDtypeStruct((m, n_out), BF16)]
    args = [a, w3, row_scale]
    side_in, side_out = [], []
    for arr, slab in sides:
        _, r, c = arr.shape
        n_slab = r // slab
        assert n_slab * slab == r and n_slab <= nj * ni and slab % 16 == 0
        side_in.append(pl.BlockSpec((None, slab, c),
                                    lambda j, i, n_slab=n_slab: (layer, jnp.minimum(j * ni + i, n_slab - 1), 0)))
        side_out.append(pl.BlockSpec((slab, c), lambda j, i, n_slab=n_slab: (jnp.minimum(j * ni + i, n_slab - 1), 0)))
        out_shape.append(jax.ShapeDtypeStruct((r, c), BF16))
        args.append(arr)
    in_specs += side_in
    out_specs += side_out
    outs = pl.pallas_call(
        functools.partial(_mm_ws_kernel, layer=layer, transposed=transposed, n_plain=n_plain,
                          seg_start=seg_start, relu2=relu2, n_side=len(sides)),
        grid=(nj, ni),
        in_specs=in_specs,
        out_specs=out_specs,
        out_shape=out_shape,
        scratch_shapes=[pltpu.VMEM(stage_shape, F32),
                        pltpu.VMEM(stage_shape, BF16),
                        pltpu.SemaphoreType.DMA(())],
        compiler_params=_cparams(("arbitrary", "arbitrary")),
        name=name,
    )(*args)
    return outs[0] if not sides else outs


def _mm_res_kernel(a_ref, w_ref, r_ref, *rest, with_norm, nk):
    if with_norm:
        g_ref, o_ref, xg_ref, rs_ref, ssq_ref = rest
    else:
        (o_ref,) = rest
    j, k = pl.program_id(1), pl.program_id(2)
    nj = pl.num_programs(1)

    def step(first, last):
        base = r_ref if first else o_ref
        x = base[...] + jnp.dot(a_ref[...], w_ref[...], preferred_element_type=F32)
        o_ref[...] = x
        if with_norm and last:
            xg_ref[...] = (x * g_ref[...]).astype(BF16)
            part = jnp.sum(x * x, axis=-1, keepdims=True)

            @pl.when(j == 0)
            def _():
                ssq_ref[...] = part

            @pl.when(j > 0)
            def _():
                ssq_ref[...] += part

            @pl.when(j == nj - 1)
            def _():
                rs_ref[...] = lax.rsqrt(ssq_ref[...] * (1.0 / (nj * o_ref.shape[1])) + EPS)

    if nk == 1:
        step(True, True)
    else:
        pl.when(k == 0)(lambda: step(True, False))
        if nk > 2:
            pl.when((k > 0) & (k < nk - 1))(lambda: step(False, False))
        pl.when(k == nk - 1)(lambda: step(False, True))


def matmul_residual(a, w, res, *, norm_g=None, tm=1024, tn=1024, tk=2048, name="mm_res"):
    m, k = a.shape
    n = w.shape[-1]
    tm, tn, tk = min(tm, m), min(tn, n), min(tk, k)
    with_norm = norm_g is not None
    tile = pl.BlockSpec((tm, tn), lambda i, j, kk: (i, j))
    in_specs = [pl.BlockSpec((tm, tk), lambda i, j, kk: (i, kk)),
                pl.BlockSpec((tk, tn), lambda i, j, kk: (kk, j)),
                tile]
    out_specs, out_shape, scratch, args = [tile], [jax.ShapeDtypeStruct((m, n), F32)], [], [a, w, res]
    if with_norm:
        in_specs.append(pl.BlockSpec((1, tn), lambda i, j, kk: (0, j)))
        out_specs += [tile, pl.BlockSpec((tm, 1), lambda i, j, kk: (i, 0))]
        out_shape += [jax.ShapeDtypeStruct((m, n), BF16), jax.ShapeDtypeStruct((m, 1), F32)]
        scratch.append(pltpu.VMEM((tm, 1), F32))
        args.append(norm_g.reshape(1, n))
    outs = pl.pallas_call(
        functools.partial(_mm_res_kernel, with_norm=with_norm, nk=k // tk),
        grid=(m // tm, n // tn, k // tk),
        in_specs=in_specs,
        out_specs=out_specs,
        out_shape=out_shape,
        scratch_shapes=scratch,
        compiler_params=_cparams(("parallel", "arbitrary", "arbitrary")),
        name=name,
    )(*args)
    return outs if with_norm else outs[0]


def _mm_merge_kernel(a1_ref, w1_ref, a2_ref, w2_ref, g1_ref, g2_ref, o_ref):
    u1 = jnp.dot(a1_ref[...], w1_ref[...], preferred_element_type=F32)
    u2 = jnp.dot(a2_ref[...], w2_ref[...], preferred_element_type=F32)
    y = (jax.nn.sigmoid(g1_ref[...].astype(F32)) * u1
         + jax.nn.sigmoid(g2_ref[...].astype(F32)) * u2)
    o_ref[...] = y.astype(o_ref.dtype)


def merge_up_proj(o_a, w_a, o_r, w_r, proj, *, tm=1024, tn=1024):
    m, k = o_a.shape
    n = w_a.shape[1]
    tm = min(tm, m)
    ca, cr = COL_MERGE_A // tn, COL_MERGE_R // tn
    return pl.pallas_call(
        _mm_merge_kernel,
        grid=(m // tm, n // tn),
        in_specs=[pl.BlockSpec((tm, k), lambda i, j: (i, 0)),
                  pl.BlockSpec((k, tn), lambda i, j: (0, j)),
                  pl.BlockSpec((tm, k), lambda i, j: (i, 0)),
                  pl.BlockSpec((k, tn), lambda i, j: (0, j)),
                  pl.BlockSpec((tm, tn), lambda i, j: (i, ca + j)),
                  pl.BlockSpec((tm, tn), lambda i, j: (i, cr + j))],
        out_specs=pl.BlockSpec((tm, tn), lambda i, j: (i, j)),
        out_shape=jax.ShapeDtypeStruct((m, n), BF16),
        compiler_params=_cparams(("parallel", "arbitrary")),
        name="merge_up",
    )(o_a, w_a, o_r, w_r, proj, proj)


def _t5_bucket(dist):
    n = jnp.maximum(dist, 0)
    max_exact = N_BUCKETS // 2
    large = max_exact + (jnp.log(jnp.maximum(n, 1).astype(F32) / max_exact)
                         / math.log(MAX_DISTANCE / max_exact) * (N_BUCKETS - max_exact)).astype(jnp.int32)
    large = jnp.minimum(large, N_BUCKETS - 1)
    return jnp.where(n < max_exact, n, large)


N_PAIR_TABLES = 6
ZERO_PAIR_TABLE = 4


def _bucket_maps():
    kk = jnp.arange(Q_BLOCK)[:, None]
    r = jnp.arange(Q_BLOCK)[None, :]

    def tile(delta):
        dist = delta * Q_BLOCK + r - kk
        return jnp.where((dist >= 0) & (dist < WINDOW), _t5_bucket(dist), -1)

    neg = jnp.full((Q_BLOCK, Q_BLOCK), -1)
    far = jnp.full((Q_BLOCK, Q_BLOCK), N_BUCKETS - 1)
    pairs = [(tile(0), neg), (far, tile(1)), (tile(4), far),
             (tile(1), tile(0)), (far, far), (neg, tile(4))]
    pair_map = jnp.concatenate([jnp.concatenate(p, axis=0) for p in pairs], axis=0).astype(jnp.int32)
    mrow = jnp.arange(CMP_NEAR)[:, None]
    dist_c = r - CMP_STRIDE * (mrow - CMP_PER_QB) - (CMP_LEN - 1)
    cmp_rows = jnp.where(dist_c >= 0, _t5_bucket(dist_c), -1)
    cmp_map = jnp.concatenate([cmp_rows, jnp.full((8, Q_BLOCK), N_BUCKETS - 1)], axis=0).astype(jnp.int32)
    return pair_map, cmp_map


def _bias_kernel(relb_ref, pmap_ref, cmap_ref, tp_ref, cb_ref, br_ref):
    h = pl.program_id(0)
    far = relb_ref[N_BUCKETS - 1, h]

    def lookup(bmap, offset):
        out = jnp.full(bmap.shape, NEG_INF, F32)
        for b in range(N_BUCKETS):
            out = jnp.where(bmap == b, (relb_ref[b, h] - offset) * LOG2E, out)
        return out

    tp_ref[...] = lookup(pmap_ref[...], far)
    cb_ref[...] = lookup(cmap_ref[...], 0.0)
    full = jnp.full(br_ref.shape, far * LOG2E, F32)
    hi = full.astype(BF16).astype(F32)
    row = lax.broadcasted_iota(jnp.int32, br_ref.shape, 0)
    br_ref[...] = jnp.where(row == 0, hi, jnp.where(row == 1, full - hi, 0.0))


def bias_tables(rel_bias):
    pair_map, cmp_map = _bucket_maps()
    rp, rc = pair_map.shape[0], cmp_map.shape[0]

    def out_spec(rows):
        return pl.BlockSpec((None, rows, LANES), lambda h: (h // NSA_GROUP, 0, h % NSA_GROUP))

    tp, cb, br = pl.pallas_call(
        _bias_kernel,
        grid=(NSA_HEADS,),
        in_specs=[pl.BlockSpec(memory_space=pltpu.SMEM),
                  pl.BlockSpec((rp, LANES), lambda h: (0, 0)),
                  pl.BlockSpec((rc, LANES), lambda h: (0, 0))],
        out_specs=[out_spec(rp), out_spec(rc), out_spec(16)],
        out_shape=[jax.ShapeDtypeStruct((NSA_KV_HEADS, rp, GQ), F32),
                   jax.ShapeDtypeStruct((NSA_KV_HEADS, rc, GQ), F32),
                   jax.ShapeDtypeStruct((NSA_KV_HEADS, 16, GQ), F32)],
        compiler_params=_cparams(("arbitrary",)),
        name="bias_tables",
    )(rel_bias, pair_map, cmp_map)
    return tp.reshape(NSA_KV_HEADS, N_PAIR_TABLES, KT, GQ), cb, br


def _kv_prep_kernel(ks_ref, vs_ref, kw_ref, vw_ref, g_ref, kso_ref, vso_ref, kwo_ref, vwo_ref):
    p = pl.program_id(2)

    def norm(x, g):
        ms = jnp.mean(x * x, axis=-1, keepdims=True)
        return x * lax.rsqrt(ms + EPS) * g

    lane = lax.broadcasted_iota(jnp.int32, (KT, LANES), 1)
    row = lax.broadcasted_iota(jnp.int32, (KT, LANES), 0)
    ones = jnp.where(lane < 2, 1.0, 0.0)
    blk = (KT // SEL_LEN) * p + jnp.right_shift(row, 6)
    onehot = jnp.where(lane == blk + (SEL_ROW0 - HEAD_DIM), 1.0, 0.0)
    ks = norm(ks_ref[...].astype(F32), g_ref[1:2, :])
    kw = norm(kw_ref[...].astype(F32), g_ref[2:3, :])
    kso_ref[...] = jnp.concatenate([ks, ones + onehot], axis=1).astype(BF16)
    kwo_ref[...] = jnp.concatenate([kw, ones], axis=1).astype(BF16)
    vso_ref[...] = vs_ref[...].astype(F32).T.astype(BF16)
    vwo_ref[...] = vw_ref[...].astype(F32).T.astype(BF16)


def kv_prep(proj, k_norm_g, B, T):
    npair = T // KT
    G = NSA_KV_HEADS
    assert SEL_LEN == 64 and SEL_ROW0 + T // SEL_LEN <= K_AUG

    def col(c0):
        return pl.BlockSpec((KT, HEAD_DIM), lambda b, g, p: (b * npair + p, c0 // HEAD_DIM + g))

    k_spec = pl.BlockSpec((None, None, KT, K_AUG), lambda b, g, p: (b, g, p, 0))
    v_spec = pl.BlockSpec((None, None, None, HEAD_DIM, KT), lambda b, g, p: (b, g, p, 0, 0))
    k_shape = jax.ShapeDtypeStruct((B, G, T, K_AUG), BF16)
    v_shape = jax.ShapeDtypeStruct((B, G, npair, HEAD_DIM, KT), BF16)
    return pl.pallas_call(
        _kv_prep_kernel,
        grid=(B, G, npair),
        in_specs=[col(COL_KSL), col(COL_VSL), col(COL_KWN), col(COL_VWN),
                  pl.BlockSpec((3, HEAD_DIM), lambda b, g, p: (0, 0))],
        out_specs=[k_spec, v_spec, k_spec, v_spec],
        out_shape=[k_shape, v_shape, k_shape, v_shape],
        compiler_params=_cparams(("parallel", "parallel", "parallel")),
        name="kv_prep",
    )(proj, proj, proj, proj, k_norm_g)


def _compress_kernel(ck_ref, cv_ref, pos_ref, w1_ref, w2_ref, g_ref, kc_ref, vct_ref):
    nchunk = ck_ref.shape[0]
    half = CMP_STRIDE * HEAD_DIM

    def mlp(x_ref, idx):
        x = x_ref[...].astype(F32)
        xa = (x + pos_ref[idx, :, 0:half]).astype(BF16)
        xb = (x + pos_ref[idx, :, half:2 * half]).astype(BF16)
        u = jnp.dot(xa, w1_ref[idx, 0:half, :], preferred_element_type=F32)
        v = jnp.dot(xb, w1_ref[idx, half:2 * half, :], preferred_element_type=F32)
        pre = u + pltpu.roll(v, nchunk - 1, 0)
        return jnp.dot(jax.nn.gelu(pre).astype(BF16), w2_ref[idx], preferred_element_type=F32)

    kc = mlp(ck_ref, 0)
    ms = jnp.mean(kc * kc, axis=-1, keepdims=True)
    kc_ref[...] = (kc * lax.rsqrt(ms + EPS) * g_ref[0:1, :]).astype(BF16)
    vct_ref[...] = mlp(cv_ref, 1).T.astype(BF16)


def compress_kv(proj, cmp_pos, cmp_w1, cmp_w2, k_norm_g, B, T):
    G = NSA_KV_HEADS
    nchunk = T // CMP_STRIDE
    half = CMP_STRIDE * HEAD_DIM
    ckv = proj[:, COL_KC:COL_KC + 2 * D_KV].reshape(B, nchunk, CMP_STRIDE, 2, G, HEAD_DIM)
    ckv = ckv.transpose(3, 0, 4, 1, 2, 5).reshape(2, B, G, nchunk, half)
    pos = cmp_pos.reshape(2, 1, CMP_LEN * HEAD_DIM)
    in_spec = pl.BlockSpec((None, None, nchunk, half), lambda b, g: (b, g, 0, 0))
    return pl.pallas_call(
        _compress_kernel,
        grid=(B, G),
        in_specs=[in_spec, in_spec,
                  pl.BlockSpec((2, 1, CMP_LEN * HEAD_DIM), lambda b, g: (0, 0, 0)),
                  pl.BlockSpec((2, CMP_LEN * HEAD_DIM, CMP_HIDDEN), lambda b, g: (0, 0, 0)),
                  pl.BlockSpec((2, CMP_HIDDEN, HEAD_DIM), lambda b, g: (0, 0, 0)),
                  pl.BlockSpec((3, HEAD_DIM), lambda b, g: (0, 0))],
        out_specs=[pl.BlockSpec((None, None, nchunk, HEAD_DIM), lambda b, g: (b, g, 0, 0)),
                   pl.BlockSpec((None, None, HEAD_DIM, nchunk), lambda b, g: (b, g, 0, 0))],
        out_shape=[jax.ShapeDtypeStruct((B, G, nchunk, HEAD_DIM), BF16),
                   jax.ShapeDtypeStruct((B, G, HEAD_DIM, nchunk), BF16)],
        compiler_params=_cparams(("parallel", "parallel")),
        name="compress_kv",
    )(ckv[0], ckv[1], pos, cmp_w1.astype(BF16), cmp_w2.astype(BF16), k_norm_g)


def _nsa_kernel(q_ref, gate_ref, kc_ref, vct_ref, ks_ref, vst_ref, kw_ref, vwt_ref,
                tp_ref, cbt_ref, brow_ref, qg_ref, ovl_ref, o_ref, acc_ref, qa_ref, cb_ref, sa_ref, sb_ref,
                *, top_n):
    i = pl.program_id(2)
    parity = jnp.bitwise_and(i, 1)
    pd = jnp.right_shift(i, 1)
    nchunk = kc_ref.shape[0]
    n_s = ovl_ref.shape[0]
    scale = HEAD_DIM ** -0.5 * LOG2E

    q = q_ref[...].astype(F32)
    qt = jnp.concatenate([q[:, h * HEAD_DIM:(h + 1) * HEAD_DIM].T for h in range(NSA_GROUP)], axis=1)
    ms = jnp.mean(qt * qt, axis=0, keepdims=True)
    q2t = (qt * lax.rsqrt(ms + EPS) * (qg_ref[...] * scale)).astype(BF16)
    qa_ref[0:HEAD_DIM, :] = q2t
    qa_ref[BIAS_ROW0:SEL_ROW0, :] = brow_ref[...].astype(BF16)
    qa_ref[SEL_ROW0:K_AUG, :] = jnp.zeros((K_AUG - SEL_ROW0, GQ), BF16)

    crow = lax.broadcasted_iota(jnp.int32, (nchunk, GQ), 0)
    near0 = CMP_PER_QB * i - CMP_PER_QB
    cb_ref[...] = jnp.where(crow < near0, cbt_ref[CMP_NEAR:CMP_NEAR + 1, :], NEG_INF)

    @pl.when(i == 0)
    def _():
        cb_ref[0:CMP_PER_QB, :] = cbt_ref[CMP_PER_QB:CMP_NEAR, :]

    @pl.when(i > 0)
    def _():
        cb_ref[pl.ds(pl.multiple_of(near0, CMP_PER_QB), CMP_NEAR), :] = cbt_ref[0:CMP_NEAR, :]

    lc = jnp.dot(kc_ref[...], q2t, preferred_element_type=F32) + cb_ref[...]
    mc = jnp.max(lc, axis=0, keepdims=True)
    ec = jnp.where(lc > 0.5 * NEG_INF, jnp.exp2(lc - mc), 0.0)
    pc = ec * (1.0 / jnp.maximum(jnp.sum(ec, axis=0, keepdims=True), 1e-30))
    o_cmp = jnp.dot(vct_ref[...], pc.astype(BF16), preferred_element_type=F32)

    pt = pc[:, 0:Q_BLOCK]
    for h in range(1, NSA_GROUP):
        pt = pt + pc[:, h * Q_BLOCK:(h + 1) * Q_BLOCK]
    ovl = ovl_ref[...]
    p1 = pt.astype(BF16)
    r1 = pt - p1.astype(F32)
    p2 = r1.astype(BF16)
    p3 = (r1 - p2.astype(F32)).astype(BF16)
    imp = (jnp.dot(ovl, p1, preferred_element_type=F32)
           + jnp.dot(ovl, p2, preferred_element_type=F32)
           + jnp.dot(ovl, p3, preferred_element_type=F32))
    sidx = lax.broadcasted_iota(jnp.int32, (n_s, Q_BLOCK), 0)
    rq = lax.broadcasted_iota(jnp.int32, (n_s, Q_BLOCK), 1)
    cur = (Q_BLOCK // SEL_LEN) * i + jnp.where(rq >= SEL_LEN, 1, 0)
    forced = (sidx == 0) | (sidx == cur) | (sidx == cur - 1)
    score = jnp.where(forced, FORCE_SCORE, jnp.where(sidx > cur, NEG_INF, imp))
    rank = jnp.zeros((n_s, Q_BLOCK), F32)
    for sp in range(n_s):
        row = score[sp:sp + 1, :]
        rank = rank + jnp.where(sidx > sp, jnp.where(row >= score, 1.0, 0.0), jnp.where(row > score, 1.0, 0.0))
    selb = jnp.where(rank < top_n, 0.0, NEG_INF)
    qa_ref[SEL_ROW0:SEL_ROW0 + n_s, :] = jnp.concatenate([selb] * NSA_GROUP, axis=1).astype(BF16)

    def flash(k_ref, vt_ref, count, pair_of, table_of):
        last = count - 1

        def logits(t):
            p = pair_of(jnp.minimum(t, last))
            k = k_ref[pl.ds(pl.multiple_of(p * KT, KT), KT), :]
            return jnp.dot(k, qa_ref[...], preferred_element_type=F32) + tp_ref[table_of(p)]

        def consume(s_ref, t, carry):
            m, l = carry
            s = s_ref[...]
            m_new = jnp.maximum(m, jnp.max(s, axis=0, keepdims=True))
            alpha = jnp.exp2(m - m_new)
            pe = jnp.exp2(s - m_new)
            l = alpha * l + jnp.sum(pe, axis=0, keepdims=True)
            acc_ref[...] = acc_ref[...] * alpha + jnp.dot(vt_ref[pair_of(t)], pe.astype(BF16),
                                                          preferred_element_type=F32)
            return m_new, l

        def two_pairs(u, carry):
            t = 2 * u
            carry = consume(sa_ref, t, carry)
            sb_ref[...] = logits(t + 1)
            carry = consume(sb_ref, t + 1, carry)
            sa_ref[...] = logits(t + 2)
            return carry

        acc_ref[...] = jnp.zeros_like(acc_ref)
        sa_ref[...] = logits(0)
        carry = (jnp.full((1, GQ), NEG_INF, F32), jnp.zeros((1, GQ), F32))
        carry = lax.fori_loop(0, jnp.right_shift(count, 1), two_pairs, carry)
        _, l = lax.cond(jnp.bitwise_and(count, 1) == 1, lambda c: consume(sa_ref, last, c), lambda c: c, carry)
        return acc_ref[...] * (1.0 / l)

    def near_table(p):
        return 3 * parity + (pd - p)

    o_sel = flash(ks_ref, vst_ref, pd + 1, lambda t: t,
                  lambda p: jnp.where(pd - p <= 1, near_table(p), ZERO_PAIR_TABLE))

    o_win = flash(kw_ref, vwt_ref, jnp.minimum(pd, 2) + 1, lambda t: pd - t, near_table)

    gt = jax.nn.sigmoid(gate_ref[...].astype(F32)).T
    first_group = pl.program_id(1) == 0

    def gate_row(br):
        rows = []
        for h in range(NSA_GROUP):
            lo = 3 * h + br
            hi = 3 * (NSA_GROUP + h) + br
            rows.append(jnp.where(first_group, gt[lo:lo + 1, :], gt[hi:hi + 1, :]))
        return jnp.concatenate(rows, axis=1)

    ot = gate_row(0) * o_cmp + gate_row(1) * o_sel + gate_row(2) * o_win
    for h in range(NSA_GROUP):
        o_ref[:, h * HEAD_DIM:(h + 1) * HEAD_DIM] = ot[:, h * Q_BLOCK:(h + 1) * Q_BLOCK].T.astype(o_ref.dtype)


def _overlap_t(T):
    n_c = T // CMP_STRIDE
    n_s = T // SEL_LEN
    c_start = jnp.arange(n_c) * CMP_STRIDE
    c_end = c_start + CMP_LEN - 1
    s_start = jnp.arange(n_s) * SEL_LEN
    ov = (c_start[None, :] < s_start[:, None] + SEL_LEN) & (c_end[None, :] >= s_start[:, None])
    ov = ov & (jnp.arange(n_c)[None, :] < n_c - 1)
    return ov.astype(BF16)


def nsa_attention(proj, q_norm_g, kc, vct, ks, vst, kw, vwt, tp, cbt, brow, B, T):
    nb = T // Q_BLOCK
    npair = T // KT
    G = NSA_KV_HEADS
    nchunk = T // CMP_STRIDE
    n_s = T // SEL_LEN
    top_n = min(SEL_TOPN, n_s)
    k_spec = pl.BlockSpec((None, None, T, K_AUG), lambda b, g, i: (b, g, 0, 0))
    v_spec = pl.BlockSpec((None, None, npair, HEAD_DIM, KT), lambda b, g, i: (b, g, 0, 0, 0))
    gate_blk = COL_GATE // LANES
    return pl.pallas_call(
        functools.partial(_nsa_kernel, top_n=top_n),
        grid=(B, G, nb),
        in_specs=[pl.BlockSpec((Q_BLOCK, GQ), lambda b, g, i: (b * nb + i, g)),
                  pl.BlockSpec((Q_BLOCK, LANES), lambda b, g, i: (b * nb + i, gate_blk)),
                  pl.BlockSpec((None, None, nchunk, HEAD_DIM), lambda b, g, i: (b, g, 0, 0)),
                  pl.BlockSpec((None, None, HEAD_DIM, nchunk), lambda b, g, i: (b, g, 0, 0)),
                  k_spec, v_spec, k_spec, v_spec,
                  pl.BlockSpec((None, N_PAIR_TABLES, KT, GQ), lambda b, g, i: (g, 0, 0, 0)),
                  pl.BlockSpec((None, CMP_NEAR + 8, GQ), lambda b, g, i: (g, 0, 0)),
                  pl.BlockSpec((None, 16, GQ), lambda b, g, i: (g, 0, 0)),
                  pl.BlockSpec((HEAD_DIM, 1), lambda b, g, i: (0, 0)),
                  pl.BlockSpec((n_s, nchunk), lambda b, g, i: (0, 0))],
        out_specs=pl.BlockSpec((Q_BLOCK, GQ), lambda b, g, i: (b * nb + i, g)),
        out_shape=jax.ShapeDtypeStruct((B * T, D_NSA), BF16),
        scratch_shapes=[pltpu.VMEM((HEAD_DIM, GQ), F32),
                        pltpu.VMEM((K_AUG, GQ), BF16),
                        pltpu.VMEM((nchunk, GQ), F32),
                        pltpu.VMEM((KT, GQ), F32),
                        pltpu.VMEM((KT, GQ), F32)],
        compiler_params=_cparams(("parallel", "parallel", "arbitrary")),
        name="nsa_attention",
    )(proj, proj, kc, vct, ks, vst, kw, vwt, tp, cbt, brow, q_norm_g.reshape(HEAD_DIM, 1), _overlap_t(T))


def _ret_tables(T):
    H, C = RET_HEADS, RET_CHUNK
    half = RET_HEAD_DIM // 2
    pos = jnp.arange(T, dtype=F32)
    freqs = ROPE_BASE ** (-jnp.arange(half, dtype=F32) / half)
    ang = pos[:, None] * freqs[None, :]
    log_gamma = jnp.log(1.0 - 2.0 ** (-5.0 - jnp.arange(H, dtype=F32)))
    n = jnp.arange(C, dtype=F32)
    diff = n[:, None] - n[None, :]
    decay_in = jnp.where(diff >= 0, jnp.exp(jnp.maximum(diff, 0.0)[None] * log_gamma[:, None, None]), 0.0)
    q_decay = jnp.exp((n + 1.0)[None] * log_gamma[:, None])
    k_decay = jnp.exp((C - 1.0 - n)[None] * log_gamma[:, None])
    chunk_decay = jnp.exp(C * log_gamma)
    qk_decay = jnp.stack([q_decay, k_decay], axis=1)[..., None]
    return jnp.cos(ang), jnp.sin(ang), decay_in, qk_decay, chunk_decay


def _retention_kernel(cd_ref, q_ref, k_ref, v_ref, g_ref, cos_ref, sin_ref, din_ref, qkd_ref, gn_ref,
                      o_ref, state_ref):
    c = pl.program_id(1)
    d = RET_HEAD_DIM
    half = d // 2

    @pl.when(c == 0)
    def _():
        state_ref[...] = jnp.zeros_like(state_ref)

    cos = cos_ref[...]
    sin = sin_ref[...]

    def rope(x):
        x1, x2 = x[:, :half], x[:, half:]
        return jnp.concatenate([x1 * cos - x2 * sin, x2 * cos + x1 * sin], axis=1)

    for h in range(RET_HEADS):
        sl = slice(h * d, (h + 1) * d)
        qr = rope(q_ref[:, sl].astype(F32))
        kr = rope(k_ref[:, sl].astype(F32)) * (d ** -0.5)
        v = v_ref[:, sl]
        qb = qr.astype(BF16)
        inner = lax.dot_general(qb, kr.astype(BF16), (((1,), (1,)), ((), ())), preferred_element_type=F32)
        inner = (inner * din_ref[h]).astype(BF16)
        state = state_ref[h]
        cross = jnp.dot((qr * qkd_ref[h, 0]).astype(BF16), state.astype(BF16), preferred_element_type=F32)
        o = jnp.dot(inner, v, preferred_element_type=F32) + cross
        kdt = (kr * qkd_ref[h, 1]).T.astype(BF16)
        state_ref[h] = state * cd_ref[h] + jnp.dot(kdt, v, preferred_element_type=F32)
        mu = jnp.mean(o, axis=-1, keepdims=True)
        var = jnp.mean(jnp.square(o - mu), axis=-1, keepdims=True)
        on = (o - mu) * lax.rsqrt(var + EPS) * gn_ref[:, sl]
        o_ref[:, sl] = (jax.nn.silu(g_ref[:, sl].astype(F32)) * on).astype(o_ref.dtype)


def retention(proj, gn_g, B, T):
    nc = T // RET_CHUNK
    cos, sin, decay_in, qk_decay, chunk_decay = _ret_tables(T)
    cblk = COL_RET // D_RET

    def seg(k):
        return pl.BlockSpec((RET_CHUNK, D_RET), lambda b, c: (b * nc + c, cblk + k))

    rope_spec = pl.BlockSpec((RET_CHUNK, RET_HEAD_DIM // 2), lambda b, c: (c, 0))
    return pl.pallas_call(
        _retention_kernel,
        grid=(B, nc),
        in_specs=[pl.BlockSpec(memory_space=pltpu.SMEM),
                  seg(0), seg(1), seg(2), seg(3), rope_spec, rope_spec,
                  pl.BlockSpec((RET_HEADS, RET_CHUNK, RET_CHUNK), lambda b, c: (0, 0, 0)),
                  pl.BlockSpec((RET_HEADS, 2, RET_CHUNK, 1), lambda b, c: (0, 0, 0, 0)),
                  pl.BlockSpec((1, D_RET), lambda b, c: (0, 0))],
        out_specs=pl.BlockSpec((RET_CHUNK, D_RET), lambda b, c: (b * nc + c, 0)),
        out_shape=jax.ShapeDtypeStruct((B * T, D_RET), BF16),
        scratch_shapes=[pltpu.VMEM((RET_HEADS, RET_HEAD_DIM, RET_HEAD_DIM), F32)],
        compiler_params=_cparams(("parallel", "arbitrary")),
        name="retention",
    )(chunk_decay, proj, proj, proj, proj, cos, sin, decay_in, qk_decay, gn_g.reshape(1, D_RET))


def kernel(x, norm1_g, w_in, nsa_q_norm_g, nsa_k_norm_g, cmp_pos, cmp_w1, cmp_w2, ret_gn_g,
           w_up_nsa, w_up_ret, w_out, norm2_g, w_ff1, w_ff2, rel_bias):
    B, T, D = x.shape
    depth = w_in.shape[0]
    xf = x.reshape(B * T, D)
    tp, cbt, brow = bias_tables(rel_bias)
    w_in_t = jnp.swapaxes(w_in, 1, 2)
    h, rs = rmsnorm_split(xf, norm1_g[0])
    for l in range(depth):
        proj, w_out_b, w_upa_b, w_upr_b = matmul_wcast(
            h, rs, w_in_t, l, n_out=D_PROJ, transposed=True, n_plain=COL_RET // 1024, seg_start=N_REAL_A,
            sides=((w_out, 32), (w_up_nsa, 16), (w_up_ret, 16)), name="in_proj")
        ks, vst, kw, vwt = kv_prep(proj, nsa_k_norm_g[l], B, T)
        kc, vct = compress_kv(proj, cmp_pos[l], cmp_w1[l], cmp_w2[l], nsa_k_norm_g[l], B, T)
        o_a = nsa_attention(proj, nsa_q_norm_g[l], kc, vct, ks, vst, kw, vwt, tp, cbt, brow, B, T)
        o_r = retention(proj, ret_gn_g[l], B, T)
        y = merge_up_proj(o_a, w_upa_b, o_r, w_upr_b, proj)
        xf, h2, rs2 = matmul_residual(y, w_out_b, xf, norm_g=norm2_g[l], name="out_proj")
        a, w_ff2_b = matmul_wcast(h2, rs2, w_ff1, l, n_out=D_FF, relu2=True, sides=((w_ff2, 128),), name="ffn_up")
        if l + 1 < depth:
            xf, h, rs = matmul_residual(a, w_ff2_b, xf, norm_g=norm1_g[l + 1], name="ffn_down")
        else:
            xf = matmul_residual(a, w_ff2_b, xf, name="ffn_down")
    return xf.reshape(B, T, D)
```

```python
import functools
import math

import jax
import jax.numpy as jnp
from jax import lax
from jax.experimental import pallas as pl
from jax.experimental.pallas import tpu as pltpu

F32 = jnp.float32
BF16 = jnp.bfloat16

D_MODEL = 4096
HEAD_DIM = 128
D_NSA = D_MODEL // 2
NSA_HEADS = D_NSA // HEAD_DIM
NSA_KV_HEADS = 2
NSA_GROUP = NSA_HEADS // NSA_KV_HEADS
D_KV = NSA_KV_HEADS * HEAD_DIM
CMP_LEN = 32
CMP_STRIDE = 16
CMP_HIDDEN = 256
SEL_LEN = 64
SEL_TOPN = 16
WINDOW = 512
Q_BLOCK = 128
D_RET = D_MODEL // 2
RET_HEADS = 8
RET_HEAD_DIM = D_RET // RET_HEADS
RET_CHUNK = 128
D_FF = 4 * D_MODEL
N_BUCKETS = 32
MAX_DISTANCE = 128
ROPE_BASE = 10000.0
EPS = 1e-6
NEG_INF = -1e30
FORCE_SCORE = 1e9
LOG2E = 1.4426950408889634

COL_Q = 0
COL_KC = D_NSA
COL_KSL = COL_KC + 2 * D_KV
COL_VSL = COL_KSL + D_KV
COL_KWN = COL_VSL + D_KV
COL_VWN = COL_KWN + D_KV
COL_GATE = COL_VWN + D_KV
N_GATE = 3 * NSA_HEADS
COL_RET = 4096
COL_MERGE_A = COL_RET + 4 * D_RET
COL_MERGE_R = COL_MERGE_A + D_MODEL
D_PROJ = COL_MERGE_R + D_MODEL
N_REAL_A = D_NSA + 6 * D_KV + N_GATE

LANES = 128
GQ = NSA_GROUP * Q_BLOCK
KT = 2 * Q_BLOCK
K_AUG = HEAD_DIM + LANES
BIAS_ROW0 = HEAD_DIM
SEL_ROW0 = HEAD_DIM + 16
CMP_PER_QB = Q_BLOCK // CMP_STRIDE
CMP_NEAR = 2 * CMP_PER_QB
VMEM_LIMIT = 60 * 1024 * 1024


def _cparams(sem, vmem=VMEM_LIMIT):
    return pltpu.CompilerParams(dimension_semantics=sem, vmem_limit_bytes=vmem)


def _rmsnorm_kernel(x_ref, g_ref, xg_ref, rs_ref):
    x = x_ref[...]
    xg_ref[...] = (x * g_ref[...]).astype(BF16)
    rs_ref[...] = lax.rsqrt(jnp.mean(x * x, axis=-1, keepdims=True) + EPS)


def rmsnorm_split(x, g, tm=256):
    n, d = x.shape
    return pl.pallas_call(
        _rmsnorm_kernel,
        grid=(n // tm,),
        in_specs=[pl.BlockSpec((tm, d), lambda i: (i, 0)),
                  pl.BlockSpec((1, d), lambda i: (0, 0))],
        out_specs=[pl.BlockSpec((tm, d), lambda i: (i, 0)),
                   pl.BlockSpec((tm, 1), lambda i: (i, 0))],
        out_shape=[jax.ShapeDtypeStruct((n, d), BF16), jax.ShapeDtypeStruct((n, 1), F32)],
        compiler_params=_cparams(("parallel",)),
        name="rmsnorm",
    )(x, g.reshape(1, d))


W_CAST_VREGS = 256


def _mm_ws_kernel(a_ref, w_hbm, rs_ref, *rest, layer, transposed, n_plain, seg_start, relu2, n_side):
    side_refs, rest = rest[:n_side], rest[n_side:]
    o_ref, rest = rest[0], rest[1:]
    side_o_refs, (stage_ref, wb_ref, sem) = rest[:n_side], rest[n_side:]
    for src, dst in zip(side_refs, side_o_refs):
        dst[...] = src[...].astype(BF16)
    j = pl.program_id(0)
    i = pl.program_id(1)
    nj = pl.num_programs(0)
    tn = o_ref.shape[1]

    def w_copy(jj):
        if transposed:
            start = jnp.where(jj < n_plain, jj * tn, seg_start + (jj - n_plain) * tn)
            src = w_hbm.at[layer, pl.ds(pl.multiple_of(start, 16), tn), :]
        else:
            src = w_hbm.at[layer, :, pl.ds(pl.multiple_of(jj * tn, tn), tn)]
        return pltpu.make_async_copy(src, stage_ref, sem)

    @pl.when(i == 0)
    def _():
        @pl.when(j == 0)
        def _():
            w_copy(j).start()

        w_copy(j).wait()
        rows, cols = stage_ref.shape
        step = min(rows, W_CAST_VREGS * 8 * LANES // cols)
        assert rows % step == 0

        def cast(r, _):
            sl = pl.ds(pl.multiple_of(r * step, step), step)
            wb_ref[sl, :] = stage_ref[sl, :].astype(BF16)
            return 0

        lax.fori_loop(0, rows // step, cast, 0)

        @pl.when(j + 1 < nj)
        def _():
            w_copy(j + 1).start()

    if transposed:
        acc = lax.dot_general(a_ref[...], wb_ref[...], (((1,), (1,)), ((), ())), preferred_element_type=F32)
    else:
        acc = jnp.dot(a_ref[...], wb_ref[...], preferred_element_type=F32)
    acc = acc * rs_ref[...]
    if relu2:
        acc = jnp.square(jnp.maximum(acc, 0.0))
    o_ref[...] = acc.astype(o_ref.dtype)


def matmul_wcast(a, row_scale, w3, layer, *, n_out, transposed=False, n_plain=None, seg_start=0, relu2=False,
                 sides=(), tm=1024, tn=1024, name="mm"):
    m, k = a.shape
    tm = min(tm, m)
    nj, ni = n_out // tn, m // tm
    n_plain = nj if n_plain is None else n_plain
    assert transposed or n_plain == nj
    stage_shape = (tn, k) if transposed else (k, tn)
    in_specs = [pl.BlockSpec((tm, k), lambda j, i: (i, 0)),
                pl.BlockSpec(memory_space=pl.ANY),
                pl.BlockSpec((tm, 1), lambda j, i: (i, 0))]
    out_specs = [pl.BlockSpec((tm, tn), lambda j, i: (i, j))]
    out_shape = [jax.ShapeDtypeStruct((m, n_out), BF16)]
    args = [a, w3, row_scale]
    side_in, side_out = [], []
    for arr, slab in sides:
        _, r, c = arr.shape
        n_slab = r // slab
        assert n_slab * slab == r and n_slab <= nj * ni and slab % 16 == 0
        side_in.append(pl.BlockSpec((None, slab, c),
                                    lambda j, i, n_slab=n_slab: (layer, jnp.minimum(j * ni + i, n_slab - 1), 0)))
        side_out.append(pl.BlockSpec((slab, c), lambda j, i, n_slab=n_slab: (jnp.minimum(j * ni + i, n_slab - 1), 0)))
        out_shape.append(jax.ShapeDtypeStruct((r, c), BF16))
        args.append(arr)
    in_specs += side_in
    out_specs += side_out
    outs = pl.pallas_call(
        functools.partial(_mm_ws_kernel, layer=layer, transposed=transposed, n_plain=n_plain,
                          seg_start=seg_start, relu2=relu2, n_side=len(sides)),
        grid=(nj, ni),
        in_specs=in_specs,
        out_specs=out_specs,
        out_shape=out_shape,
        scratch_shapes=[pltpu.VMEM(stage_shape, F32),
                        pltpu.VMEM(stage_shape, BF16),
                        pltpu.SemaphoreType.DMA(())],
        compiler_params=_cparams(("arbitrary", "arbitrary")),
        name=name,
    )(*args)
    return outs[0] if not sides else outs


def _mm_res_kernel(a_ref, w_ref, r_ref, *rest, with_norm, nk):
    if with_norm:
        g_ref, o_ref, xg_ref, rs_ref, ssq_ref = rest
    else:
        (o_ref,) = rest
    j, k = pl.program_id(1), pl.program_id(2)
    nj = pl.num_programs(1)

    def step(first, last):
        base = r_ref if first else o_ref
        x = base[...] + jnp.dot(a_ref[...], w_ref[...], preferred_element_type=F32)
        o_ref[...] = x
        if with_norm and last:
            xg_ref[...] = (x * g_ref[...]).astype(BF16)
            part = jnp.sum(x * x, axis=-1, keepdims=True)

            @pl.when(j == 0)
            def _():
                ssq_ref[...] = part

            @pl.when(j > 0)
            def _():
                ssq_ref[...] += part

            @pl.when(j == nj - 1)
            def _():
                rs_ref[...] = lax.rsqrt(ssq_ref[...] * (1.0 / (nj * o_ref.shape[1])) + EPS)

    if nk == 1:
        step(True, True)
    else:
        pl.when(k == 0)(lambda: step(True, False))
        if nk > 2:
            pl.when((k > 0) & (k < nk - 1))(lambda: step(False, False))
        pl.when(k == nk - 1)(lambda: step(False, True))


def matmul_residual(a, w, res, *, norm_g=None, tm=1024, tn=1024, tk=2048, name="mm_res"):
    m, k = a.shape
    n = w.shape[-1]
    tm, tn, tk = min(tm, m), min(tn, n), min(tk, k)
    with_norm = norm_g is not None
    tile = pl.BlockSpec((tm, tn), lambda i, j, kk: (i, j))
    in_specs = [pl.BlockSpec((tm, tk), lambda i, j, kk: (i, kk)),
                pl.BlockSpec((tk, tn), lambda i, j, kk: (kk, j)),
                tile]
    out_specs, out_shape, scratch, args = [tile], [jax.ShapeDtypeStruct((m, n), F32)], [], [a, w, res]
    if with_norm:
        in_specs.append(pl.BlockSpec((1, tn), lambda i, j, kk: (0, j)))
        out_specs += [tile, pl.BlockSpec((tm, 1), lambda i, j, kk: (i, 0))]
        out_shape += [jax.ShapeDtypeStruct((m, n), BF16), jax.ShapeDtypeStruct((m, 1), F32)]
        scratch.append(pltpu.VMEM((tm, 1), F32))
        args.append(norm_g.reshape(1, n))
    outs = pl.pallas_call(
        functools.partial(_mm_res_kernel, with_norm=with_norm, nk=k // tk),
        grid=(m // tm, n // tn, k // tk),
        in_specs=in_specs,
        out_specs=out_specs,
        out_shape=out_shape,
        scratch_shapes=scratch,
        compiler_params=_cparams(("parallel", "arbitrary", "arbitrary")),
        name=name,
    )(*args)
    return outs if with_norm else outs[0]


def _mm_merge_kernel(a1_ref, w1_ref, a2_ref, w2_ref, g1_ref, g2_ref, o_ref):
    u1 = jnp.dot(a1_ref[...], w1_ref[...], preferred_element_type=F32)
    u2 = jnp.dot(a2_ref[...], w2_ref[...], preferred_element_type=F32)
    y = (jax.nn.sigmoid(g1_ref[...].astype(F32)) * u1
         + jax.nn.sigmoid(g2_ref[...].astype(F32)) * u2)
    o_ref[...] = y.astype(o_ref.dtype)


def merge_up_proj(o_a, w_a, o_r, w_r, proj, *, tm=1024, tn=1024):
    m, k = o_a.shape
    n = w_a.shape[1]
    tm = min(tm, m)
    ca, cr = COL_MERGE_A // tn, COL_MERGE_R // tn
    return pl.pallas_call(
        _mm_merge_kernel,
        grid=(m // tm, n // tn),
        in_specs=[pl.BlockSpec((tm, k), lambda i, j: (i, 0)),
                  pl.BlockSpec((k, tn), lambda i, j: (0, j)),
                  pl.BlockSpec((tm, k), lambda i, j: (i, 0)),
                  pl.BlockSpec((k, tn), lambda i, j: (0, j)),
                  pl.BlockSpec((tm, tn), lambda i, j: (i, ca + j)),
                  pl.BlockSpec((tm, tn), lambda i, j: (i, cr + j))],
        out_specs=pl.BlockSpec((tm, tn), lambda i, j: (i, j)),
        out_shape=jax.ShapeDtypeStruct((m, n), BF16),
        compiler_params=_cparams(("parallel", "arbitrary")),
        name="merge_up",
    )(o_a, w_a, o_r, w_r, proj, proj)


def _t5_bucket(dist):
    n = jnp.maximum(dist, 0)
    max_exact = N_BUCKETS // 2
    large = max_exact + (jnp.log(jnp.maximum(n, 1).astype(F32) / max_exact)
                         / math.log(MAX_DISTANCE / max_exact) * (N_BUCKETS - max_exact)).astype(jnp.int32)
    large = jnp.minimum(large, N_BUCKETS - 1)
    return jnp.where(n < max_exact, n, large)


N_PAIR_TABLES = 6
ZERO_PAIR_TABLE = 4


def _bucket_maps():
    kk = jnp.arange(Q_BLOCK)[:, None]
    r = jnp.arange(Q_BLOCK)[None, :]

    def tile(delta):
        dist = delta * Q_BLOCK + r - kk
        return jnp.where((dist >= 0) & (dist < WINDOW), _t5_bucket(dist), -1)

    neg = jnp.full((Q_BLOCK, Q_BLOCK), -1)
    far = jnp.full((Q_BLOCK, Q_BLOCK), N_BUCKETS - 1)
    pairs = [(tile(0), neg), (far, tile(1)), (tile(4), far),
             (tile(1), tile(0)), (far, far), (neg, tile(4))]
    pair_map = jnp.concatenate([jnp.concatenate(p, axis=0) for p in pairs], axis=0).astype(jnp.int32)
    mrow = jnp.arange(CMP_NEAR)[:, None]
    dist_c = r - CMP_STRIDE * (mrow - CMP_PER_QB) - (CMP_LEN - 1)
    cmp_rows = jnp.where(dist_c >= 0, _t5_bucket(dist_c), -1)
    cmp_map = jnp.concatenate([cmp_rows, jnp.full((8, Q_BLOCK), N_BUCKETS - 1)], axis=0).astype(jnp.int32)
    return pair_map, cmp_map


def _bias_kernel(relb_ref, pmap_ref, cmap_ref, tp_ref, cb_ref, br_ref):
    h = pl.program_id(0)
    far = relb_ref[N_BUCKETS - 1, h]

    def lookup(bmap, offset):
        out = jnp.full(bmap.shape, NEG_INF, F32)
        for b in range(N_BUCKETS):
            out = jnp.where(bmap == b, (relb_ref[b, h] - offset) * LOG2E, out)
        return out

    tp_ref[...] = lookup(pmap_ref[...], far)
    cb_ref[...] = lookup(cmap_ref[...], 0.0)
    full = jnp.full(br_ref.shape, far * LOG2E, F32)
    hi = full.astype(BF16).astype(F32)
    row = lax.broadcasted_iota(jnp.int32, br_ref.shape, 0)
    br_ref[...] = jnp.where(row == 0, hi, jnp.where(row == 1, full - hi, 0.0))


def bias_tables(rel_bias):
    pair_map, cmp_map = _bucket_maps()
    rp, rc = pair_map.shape[0], cmp_map.shape[0]

    def out_spec(rows):
        return pl.BlockSpec((None, rows, LANES), lambda h: (h // NSA_GROUP, 0, h % NSA_GROUP))

    tp, cb, br = pl.pallas_call(
        _bias_kernel,
        grid=(NSA_HEADS,),
        in_specs=[pl.BlockSpec(memory_space=pltpu.SMEM),
                  pl.BlockSpec((rp, LANES), lambda h: (0, 0)),
                  pl.BlockSpec((rc, LANES), lambda h: (0, 0))],
        out_specs=[out_spec(rp), out_spec(rc), out_spec(16)],
        out_shape=[jax.ShapeDtypeStruct((NSA_KV_HEADS, rp, GQ), F32),
                   jax.ShapeDtypeStruct((NSA_KV_HEADS, rc, GQ), F32),
                   jax.ShapeDtypeStruct((NSA_KV_HEADS, 16, GQ), F32)],
        compiler_params=_cparams(("arbitrary",)),
        name="bias_tables",
    )(rel_bias, pair_map, cmp_map)
    return tp.reshape(NSA_KV_HEADS, N_PAIR_TABLES, KT, GQ), cb, br


def _kv_prep_kernel(ks_ref, vs_ref, kw_ref, vw_ref, g_ref, kso_ref, vso_ref, kwo_ref, vwo_ref):
    p = pl.program_id(2)

    def norm(x, g):
        ms = jnp.mean(x * x, axis=-1, keepdims=True)
        return x * lax.rsqrt(ms + EPS) * g

    lane = lax.broadcasted_iota(jnp.int32, (KT, LANES), 1)
    row = lax.broadcasted_iota(jnp.int32, (KT, LANES), 0)
    ones = jnp.where(lane < 2, 1.0, 0.0)
    blk = (KT // SEL_LEN) * p + jnp.right_shift(row, 6)
    onehot = jnp.where(lane == blk + (SEL_ROW0 - HEAD_DIM), 1.0, 0.0)
    ks = norm(ks_ref[...].astype(F32), g_ref[1:2, :])
    kw = norm(kw_ref[...].astype(F32), g_ref[2:3, :])
    kso_ref[...] = jnp.concatenate([ks, ones + onehot], axis=1).astype(BF16)
    kwo_ref[...] = jnp.concatenate([kw, ones], axis=1).astype(BF16)
    vso_ref[...] = vs_ref[...].astype(F32).T.astype(BF16)
    vwo_ref[...] = vw_ref[...].astype(F32).T.astype(BF16)


def kv_prep(proj, k_norm_g, B, T):
    npair = T // KT
    G = NSA_KV_HEADS
    assert SEL_LEN == 64 and SEL_ROW0 + T // SEL_LEN <= K_AUG

    def col(c0):
        return pl.BlockSpec((KT, HEAD_DIM), lambda b, g, p: (b * npair + p, c0 // HEAD_DIM + g))

    k_spec = pl.BlockSpec((None, None, KT, K_AUG), lambda b, g, p: (b, g, p, 0))
    v_spec = pl.BlockSpec((None, None, None, HEAD_DIM, KT), lambda b, g, p: (b, g, p, 0, 0))
    k_shape = jax.ShapeDtypeStruct((B, G, T, K_AUG), BF16)
    v_shape = jax.ShapeDtypeStruct((B, G, npair, HEAD_DIM, KT), BF16)
    return pl.pallas_call(
        _kv_prep_kernel,
        grid=(B, G, npair),
        in_specs=[col(COL_KSL), col(COL_VSL), col(COL_KWN), col(COL_VWN),
                  pl.BlockSpec((3, HEAD_DIM), lambda b, g, p: (0, 0))],
        out_specs=[k_spec, v_spec, k_spec, v_spec],
        out_shape=[k_shape, v_shape, k_shape, v_shape],
        compiler_params=_cparams(("parallel", "parallel", "parallel")),
        name="kv_prep",
    )(proj, proj, proj, proj, k_norm_g)


def _compress_kernel(ck_ref, cv_ref, pos_ref, w1_ref, w2_ref, g_ref, kc_ref, vct_ref):
    nchunk = ck_ref.shape[0]
    half = CMP_STRIDE * HEAD_DIM

    def mlp(x_ref, idx):
        x = x_ref[...].astype(F32)
        xa = (x + pos_ref[idx, :, 0:half]).astype(BF16)
        xb = (x + pos_ref[idx, :, half:2 * half]).astype(BF16)
        u = jnp.dot(xa, w1_ref[idx, 0:half, :], preferred_element_type=F32)
        v = jnp.dot(xb, w1_ref[idx, half:2 * half, :], preferred_element_type=F32)
        pre = u + pltpu.roll(v, nchunk - 1, 0)
        return jnp.dot(jax.nn.gelu(pre).astype(BF16), w2_ref[idx], preferred_element_type=F32)

    kc = mlp(ck_ref, 0)
    ms = jnp.mean(kc * kc, axis=-1, keepdims=True)
    kc_ref[...] = (kc * lax.rsqrt(ms + EPS) * g_ref[0:1, :]).astype(BF16)
    vct_ref[...] = mlp(cv_ref, 1).T.astype(BF16)


def compress_kv(proj, cmp_pos, cmp_w1, cmp_w2, k_norm_g, B, T):
    G = NSA_KV_HEADS
    nchunk = T // CMP_STRIDE
    half = CMP_STRIDE * HEAD_DIM
    ckv = proj[:, COL_KC:COL_KC + 2 * D_KV].reshape(B, nchunk, CMP_STRIDE, 2, G, HEAD_DIM)
    ckv = ckv.transpose(3, 0, 4, 1, 2, 5).reshape(2, B, G, nchunk, half)
    pos = cmp_pos.reshape(2, 1, CMP_LEN * HEAD_DIM)
    in_spec = pl.BlockSpec((None, None, nchunk, half), lambda b, g: (b, g, 0, 0))
    return pl.pallas_call(
        _compress_kernel,
        grid=(B, G),
        in_specs=[in_spec, in_spec,
                  pl.BlockSpec((2, 1, CMP_LEN * HEAD_DIM), lambda b, g: (0, 0, 0)),
                  pl.BlockSpec((2, CMP_LEN * HEAD_DIM, CMP_HIDDEN), lambda b, g: (0, 0, 0)),
                  pl.BlockSpec((2, CMP_HIDDEN, HEAD_DIM), lambda b, g: (0, 0, 0)),
                  pl.BlockSpec((3, HEAD_DIM), lambda b, g: (0, 0))],
        out_specs=[pl.BlockSpec((None, None, nchunk, HEAD_DIM), lambda b, g: (b, g, 0, 0)),
                   pl.BlockSpec((None, None, HEAD_DIM, nchunk), lambda b, g: (b, g, 0, 0))],
        out_shape=[jax.ShapeDtypeStruct((B, G, nchunk, HEAD_DIM), BF16),
                   jax.ShapeDtypeStruct((B, G, HEAD_DIM, nchunk), BF16)],
        compiler_params=_cparams(("parallel", "parallel")),
        name="compress_kv",
    )(ckv[0], ckv[1], pos, cmp_w1.astype(BF16), cmp_w2.astype(BF16), k_norm_g)


def _nsa_kernel(q_ref, gate_ref, kc_ref, vct_ref, ks_ref, vst_ref, kw_ref, vwt_ref,
                tp_ref, cbt_ref, brow_ref, qg_ref, ovl_ref, o_ref, acc_ref, qa_ref, cb_ref, sa_ref, sb_ref,
                *, top_n):
    i = pl.program_id(2)
    parity = jnp.bitwise_and(i, 1)
    pd = jnp.right_shift(i, 1)
    nchunk = kc_ref.shape[0]
    n_s = ovl_ref.shape[0]
    scale = HEAD_DIM ** -0.5 * LOG2E

    q = q_ref[...].astype(F32)
    qt = jnp.concatenate([q[:, h * HEAD_DIM:(h + 1) * HEAD_DIM].T for h in range(NSA_GROUP)], axis=1)
    ms = jnp.mean(qt * qt, axis=0, keepdims=True)
    q2t = (qt * lax.rsqrt(ms + EPS) * (qg_ref[...] * scale)).astype(BF16)
    qa_ref[0:HEAD_DIM, :] = q2t
    qa_ref[BIAS_ROW0:SEL_ROW0, :] = brow_ref[...].astype(BF16)
    qa_ref[SEL_ROW0:K_AUG, :] = jnp.zeros((K_AUG - SEL_ROW0, GQ), BF16)

    crow = lax.broadcasted_iota(jnp.int32, (nchunk, GQ), 0)
    near0 = CMP_PER_QB * i - CMP_PER_QB
    cb_ref[...] = jnp.where(crow < near0, cbt_ref[CMP_NEAR:CMP_NEAR + 1, :], NEG_INF)

    @pl.when(i == 0)
    def _():
        cb_ref[0:CMP_PER_QB, :] = cbt_ref[CMP_PER_QB:CMP_NEAR, :]

    @pl.when(i > 0)
    def _():
        cb_ref[pl.ds(pl.multiple_of(near0, CMP_PER_QB), CMP_NEAR), :] = cbt_ref[0:CMP_NEAR, :]

    lc = jnp.dot(kc_ref[...], q2t, preferred_element_type=F32) + cb_ref[...]
    mc = jnp.max(lc, axis=0, keepdims=True)
    ec = jnp.where(lc > 0.5 * NEG_INF, jnp.exp2(lc - mc), 0.0)
    pc = ec * (1.0 / jnp.maximum(jnp.sum(ec, axis=0, keepdims=True), 1e-30))
    o_cmp = jnp.dot(vct_ref[...], pc.astype(BF16), preferred_element_type=F32)

    pt = pc[:, 0:Q_BLOCK]
    for h in range(1, NSA_GROUP):
        pt = pt + pc[:, h * Q_BLOCK:(h + 1) * Q_BLOCK]
    ovl = ovl_ref[...]
    p1 = pt.astype(BF16)
    r1 = pt - p1.astype(F32)
    p2 = r1.astype(BF16)
    p3 = (r1 - p2.astype(F32)).astype(BF16)
    imp = (jnp.dot(ovl, p1, preferred_element_type=F32)
           + jnp.dot(ovl, p2, preferred_element_type=F32)
           + jnp.dot(ovl, p3, preferred_element_type=F32))
    sidx = lax.broadcasted_iota(jnp.int32, (n_s, Q_BLOCK), 0)
    rq = lax.broadcasted_iota(jnp.int32, (n_s, Q_BLOCK), 1)
    cur = (Q_BLOCK // SEL_LEN) * i + jnp.where(rq >= SEL_LEN, 1, 0)
    forced = (sidx == 0) | (sidx == cur) | (sidx == cur - 1)
    score = jnp.where(forced, FORCE_SCORE, jnp.where(sidx > cur, NEG_INF, imp))
    rank = jnp.zeros((n_s, Q_BLOCK), F32)
    for sp in range(n_s):
        row = score[sp:sp + 1, :]
        rank = rank + jnp.where(sidx > sp, jnp.where(row >= score, 1.0, 0.0), jnp.where(row > score, 1.0, 0.0))
    selb = jnp.where(rank < top_n, 0.0, NEG_INF)
    qa_ref[SEL_ROW0:SEL_ROW0 + n_s, :] = jnp.concatenate([selb] * NSA_GROUP, axis=1).astype(BF16)

    def flash(k_ref, vt_ref, count, pair_of, table_of):
        last = count - 1

        def logits(t):
            p = pair_of(jnp.minimum(t, last))
            k = k_ref[pl.ds(pl.multiple_of(p * KT, KT), KT), :]
            return jnp.dot(k, qa_ref[...], preferred_element_type=F32) + tp_ref[table_of(p)]

        def consume(s_ref, t, carry):
            m, l = carry
            s = s_ref[...]
            m_new = jnp.maximum(m, jnp.max(s, axis=0, keepdims=True))
            alpha = jnp.exp2(m - m_new)
            pe = jnp.exp2(s - m_new)
            l = alpha * l + jnp.sum(pe, axis=0, keepdims=True)
            acc_ref[...] = acc_ref[...] * alpha + jnp.dot(vt_ref[pair_of(t)], pe.astype(BF16),
                                                          preferred_element_type=F32)
            return m_new, l

        def two_pairs(u, carry):
            t = 2 * u
            sb_ref[...] = logits(t + 1)
            carry = consume(sa_ref, t, carry)
            sa_ref[...] = logits(t + 2)
            return consume(sb_ref, t + 1, carry)

        acc_ref[...] = jnp.zeros_like(acc_ref)
        sa_ref[...] = logits(0)
        carry = (jnp.full((1, GQ), NEG_INF, F32), jnp.zeros((1, GQ), F32))
        carry = lax.fori_loop(0, jnp.right_shift(count, 1), two_pairs, carry)
        _, l = lax.cond(jnp.bitwise_and(count, 1) == 1, lambda c: consume(sa_ref, last, c), lambda c: c, carry)
        return acc_ref[...] * (1.0 / l)

    def near_table(p):
        return 3 * parity + (pd - p)

    o_sel = flash(ks_ref, vst_ref, pd + 1, lambda t: t,
                  lambda p: jnp.where(pd - p <= 1, near_table(p), ZERO_PAIR_TABLE))

    o_win = flash(kw_ref, vwt_ref, jnp.minimum(pd, 2) + 1, lambda t: pd - t, near_table)

    gt = jax.nn.sigmoid(gate_ref[...].astype(F32)).T
    first_group = pl.program_id(1) == 0

    def gate_row(br):
        rows = []
        for h in range(NSA_GROUP):
            lo = 3 * h + br
            hi = 3 * (NSA_GROUP + h) + br
            rows.append(jnp.where(first_group, gt[lo:lo + 1, :], gt[hi:hi + 1, :]))
        return jnp.concatenate(rows, axis=1)

    ot = gate_row(0) * o_cmp + gate_row(1) * o_sel + gate_row(2) * o_win
    for h in range(NSA_GROUP):
        o_ref[:, h * HEAD_DIM:(h + 1) * HEAD_DIM] = ot[:, h * Q_BLOCK:(h + 1) * Q_BLOCK].T.astype(o_ref.dtype)


def _overlap_t(T):
    n_c = T // CMP_STRIDE
    n_s = T // SEL_LEN
    c_start = jnp.arange(n_c) * CMP_STRIDE
    c_end = c_start + CMP_LEN - 1
    s_start = jnp.arange(n_s) * SEL_LEN
    ov = (c_start[None, :] < s_start[:, None] + SEL_LEN) & (c_end[None, :] >= s_start[:, None])
    ov = ov & (jnp.arange(n_c)[None, :] < n_c - 1)
    return ov.astype(BF16)


def nsa_attention(proj, q_norm_g, kc, vct, ks, vst, kw, vwt, tp, cbt, brow, B, T):
    nb = T // Q_BLOCK
    npair = T // KT
    G = NSA_KV_HEADS
    nchunk = T // CMP_STRIDE
    n_s = T // SEL_LEN
    top_n = min(SEL_TOPN, n_s)
    k_spec = pl.BlockSpec((None, None, T, K_AUG), lambda b, g, i: (b, g, 0, 0))
    v_spec = pl.BlockSpec((None, None, npair, HEAD_DIM, KT), lambda b, g, i: (b, g, 0, 0, 0))
    gate_blk = COL_GATE // LANES
    return pl.pallas_call(
        functools.partial(_nsa_kernel, top_n=top_n),
        grid=(B, G, nb),
        in_specs=[pl.BlockSpec((Q_BLOCK, GQ), lambda b, g, i: (b * nb + i, g)),
                  pl.BlockSpec((Q_BLOCK, LANES), lambda b, g, i: (b * nb + i, gate_blk)),
                  pl.BlockSpec((None, None, nchunk, HEAD_DIM), lambda b, g, i: (b, g, 0, 0)),
                  pl.BlockSpec((None, None, HEAD_DIM, nchunk), lambda b, g, i: (b, g, 0, 0)),
                  k_spec, v_spec, k_spec, v_spec,
                  pl.BlockSpec((None, N_PAIR_TABLES, KT, GQ), lambda b, g, i: (g, 0, 0, 0)),
                  pl.BlockSpec((None, CMP_NEAR + 8, GQ), lambda b, g, i: (g, 0, 0)),
                  pl.BlockSpec((None, 16, GQ), lambda b, g, i: (g, 0, 0)),
                  pl.BlockSpec((HEAD_DIM, 1), lambda b, g, i: (0, 0)),
                  pl.BlockSpec((n_s, nchunk), lambda b, g, i: (0, 0))],
        out_specs=pl.BlockSpec((Q_BLOCK, GQ), lambda b, g, i: (b * nb + i, g)),
        out_shape=jax.ShapeDtypeStruct((B * T, D_NSA), BF16),
        scratch_shapes=[pltpu.VMEM((HEAD_DIM, GQ), F32),
                        pltpu.VMEM((K_AUG, GQ), BF16),
                        pltpu.VMEM((nchunk, GQ), F32),
                        pltpu.VMEM((KT, GQ), F32),
                        pltpu.VMEM((KT, GQ), F32)],
        compiler_params=_cparams(("parallel", "parallel", "arbitrary")),
        name="nsa_attention",
    )(proj, proj, kc, vct, ks, vst, kw, vwt, tp, cbt, brow, q_norm_g.reshape(HEAD_DIM, 1), _overlap_t(T))


def _ret_tables(T):
    H, C = RET_HEADS, RET_CHUNK
    half = RET_HEAD_DIM // 2
    pos = jnp.arange(T, dtype=F32)
    freqs = ROPE_BASE ** (-jnp.arange(half, dtype=F32) / half)
    ang = pos[:, None] * freqs[None, :]
    log_gamma = jnp.log(1.0 - 2.0 ** (-5.0 - jnp.arange(H, dtype=F32)))
    n = jnp.arange(C, dtype=F32)
    diff = n[:, None] - n[None, :]
    decay_in = jnp.where(diff >= 0, jnp.exp(jnp.maximum(diff, 0.0)[None] * log_gamma[:, None, None]), 0.0)
    q_decay = jnp.exp((n + 1.0)[None] * log_gamma[:, None])
    k_decay = jnp.exp((C - 1.0 - n)[None] * log_gamma[:, None])
    chunk_decay = jnp.exp(C * log_gamma)
    qk_decay = jnp.stack([q_decay, k_decay], axis=1)[..., None]
    return jnp.cos(ang), jnp.sin(ang), decay_in, qk_decay, chunk_decay


def _retention_kernel(cd_ref, q_ref, k_ref, v_ref, g_ref, cos_ref, sin_ref, din_ref, qkd_ref, gn_ref,
                      o_ref, state_ref):
    c = pl.program_id(1)
    d = RET_HEAD_DIM
    half = d // 2

    @pl.when(c == 0)
    def _():
        state_ref[...] = jnp.zeros_like(state_ref)

    cos = cos_ref[...]
    sin = sin_ref[...]

    def rope(x):
        x1, x2 = x[:, :half], x[:, half:]
        return jnp.concatenate([x1 * cos - x2 * sin, x2 * cos + x1 * sin], axis=1)

    for h in range(RET_HEADS):
        sl = slice(h * d, (h + 1) * d)
        qr = rope(q_ref[:, sl].astype(F32))
        kr = rope(k_ref[:, sl].astype(F32)) * (d ** -0.5)
        v = v_ref[:, sl]
        qb = qr.astype(BF16)
        inner = lax.dot_general(qb, kr.astype(BF16), (((1,), (1,)), ((), ())), preferred_element_type=F32)
        inner = (inner * din_ref[h]).astype(BF16)
        state = state_ref[h]
        cross = jnp.dot((qr * qkd_ref[h, 0]).astype(BF16), state.astype(BF16), preferred_element_type=F32)
        o = jnp.dot(inner, v, preferred_element_type=F32) + cross
        kdt = (kr * qkd_ref[h, 1]).T.astype(BF16)
        state_ref[h] = state * cd_ref[h] + jnp.dot(kdt, v, preferred_element_type=F32)
        mu = jnp.mean(o, axis=-1, keepdims=True)
        var = jnp.mean(jnp.square(o - mu), axis=-1, keepdims=True)
        on = (o - mu) * lax.rsqrt(var + EPS) * gn_ref[:, sl]
        o_ref[:, sl] = (jax.nn.silu(g_ref[:, sl].astype(F32)) * on).astype(o_ref.dtype)


def retention(proj, gn_g, B, T):
    nc = T // RET_CHUNK
    cos, sin, decay_in, qk_decay, chunk_decay = _ret_tables(T)
    cblk = COL_RET // D_RET

    def seg(k):
        return pl.BlockSpec((RET_CHUNK, D_RET), lambda b, c: (b * nc + c, cblk + k))

    rope_spec = pl.BlockSpec((RET_CHUNK, RET_HEAD_DIM // 2), lambda b, c: (c, 0))
    return pl.pallas_call(
        _retention_kernel,
        grid=(B, nc),
        in_specs=[pl.BlockSpec(memory_space=pltpu.SMEM),
                  seg(0), seg(1), seg(2), seg(3), rope_spec, rope_spec,
                  pl.BlockSpec((RET_HEADS, RET_CHUNK, RET_CHUNK), lambda b, c: (0, 0, 0)),
                  pl.BlockSpec((RET_HEADS, 2, RET_CHUNK, 1), lambda b, c: (0, 0, 0, 0)),
                  pl.BlockSpec((1, D_RET), lambda b, c: (0, 0))],
        out_specs=pl.BlockSpec((RET_CHUNK, D_RET), lambda b, c: (b * nc + c, 0)),
        out_shape=jax.ShapeDtypeStruct((B * T, D_RET), BF16),
        scratch_shapes=[pltpu.VMEM((RET_HEADS, RET_HEAD_DIM, RET_HEAD_DIM), F32)],
        compiler_params=_cparams(("parallel", "arbitrary")),
        name="retention",
    )(chunk_decay, proj, proj, proj, proj, cos, sin, decay_in, qk_decay, gn_g.reshape(1, D_RET))


def kernel(x, norm1_g, w_in, nsa_q_norm_g, nsa_k_norm_g, cmp_pos, cmp_w1, cmp_w2, ret_gn_g,
           w_up_nsa, w_up_ret, w_out, norm2_g, w_ff1, w_ff2, rel_bias):
    B, T, D = x.shape
    depth = w_in.shape[0]
    xf = x.reshape(B * T, D)
    tp, cbt, brow = bias_tables(rel_bias)
    w_in_t = jnp.swapaxes(w_in, 1, 2)
    h, rs = rmsnorm_split(xf, norm1_g[0])
    for l in range(depth):
        proj, w_out_b, w_upa_b, w_upr_b = matmul_wcast(
            h, rs, w_in_t, l, n_out=D_PROJ, transposed=True, n_plain=COL_RET // 1024, seg_start=N_REAL_A,
            sides=((w_out, 32), (w_up_nsa, 16), (w_up_ret, 16)), name="in_proj")
        ks, vst, kw, vwt = kv_prep(proj, nsa_k_norm_g[l], B, T)
        kc, vct = compress_kv(proj, cmp_pos[l], cmp_w1[l], cmp_w2[l], nsa_k_norm_g[l], B, T)
        o_a = nsa_attention(proj, nsa_q_norm_g[l], kc, vct, ks, vst, kw, vwt, tp, cbt, brow, B, T)
        o_r = retention(proj, ret_gn_g[l], B, T)
        y = merge_up_proj(o_a, w_upa_b, o_r, w_upr_b, proj)
        xf, h2, rs2 = matmul_residual(y, w_out_b, xf, norm_g=norm2_g[l], name="out_proj")
        a, w_ff2_b = matmul_wcast(h2, rs2, w_ff1, l, n_out=D_FF, relu2=True, sides=((w_ff2, 128),), name="ffn_up")
        if l + 1 < depth:
            xf, h, rs = matmul_residual(a, w_ff2_b, xf, norm_g=norm1_g[l + 1], name="ffn_down")
        else:
            xf = matmul_residual(a, w_ff2_b, xf, tk=4096, name="ffn_down")
    return xf.reshape(B, T, D)
```

```python
import functools
import math

import jax
import jax.numpy as jnp
from jax import lax
from jax.experimental import pallas as pl
from jax.experimental.pallas import tpu as pltpu

F32 = jnp.float32
BF16 = jnp.bfloat16

D_MODEL = 4096
HEAD_DIM = 128
D_NSA = D_MODEL // 2
NSA_HEADS = D_NSA // HEAD_DIM
NSA_KV_HEADS = 2
NSA_GROUP = NSA_HEADS // NSA_KV_HEADS
D_KV = NSA_KV_HEADS * HEAD_DIM
CMP_LEN = 32
CMP_STRIDE = 16
CMP_HIDDEN = 256
SEL_LEN = 64
SEL_TOPN = 16
WINDOW = 512
Q_BLOCK = 128
D_RET = D_MODEL // 2
RET_HEADS = 8
RET_HEAD_DIM = D_RET // RET_HEADS
RET_CHUNK = 128
D_FF = 4 * D_MODEL
N_BUCKETS = 32
MAX_DISTANCE = 128
ROPE_BASE = 10000.0
EPS = 1e-6
NEG_INF = -1e30
FORCE_SCORE = 1e9
LOG2E = 1.4426950408889634

COL_Q = 0
COL_KC = D_NSA
COL_KSL = COL_KC + 2 * D_KV
COL_VSL = COL_KSL + D_KV
COL_KWN = COL_VSL + D_KV
COL_VWN = COL_KWN + D_KV
COL_GATE = COL_VWN + D_KV
N_GATE = 3 * NSA_HEADS
COL_RET = 4096
COL_MERGE_A = COL_RET + 4 * D_RET
COL_MERGE_R = COL_MERGE_A + D_MODEL
D_PROJ = COL_MERGE_R + D_MODEL
N_REAL_A = D_NSA + 6 * D_KV + N_GATE

LANES = 128
GQ = NSA_GROUP * Q_BLOCK
KT = 2 * Q_BLOCK
K_AUG = HEAD_DIM + LANES
BIAS_ROW0 = HEAD_DIM
BIAS_ROWS = 16
SEL_ROW0 = BIAS_ROW0 + BIAS_ROWS
CMP_PER_QB = Q_BLOCK // CMP_STRIDE
CMP_NEAR = 2 * CMP_PER_QB
VMEM_LIMIT = 63 * 1024 * 1024


def _cparams(sem, vmem=VMEM_LIMIT):
    return pltpu.CompilerParams(dimension_semantics=sem, vmem_limit_bytes=vmem)


def _rmsnorm_kernel(x_ref, g_ref, xg_ref, rs_ref):
    x = x_ref[...]
    xg_ref[...] = (x * g_ref[...]).astype(BF16)
    rs_ref[...] = lax.rsqrt(jnp.mean(x * x, axis=-1, keepdims=True) + EPS)


def rmsnorm_split(x, g, tm=256):
    n, d = x.shape
    return pl.pallas_call(
        _rmsnorm_kernel,
        grid=(n // tm,),
        in_specs=[pl.BlockSpec((tm, d), lambda i: (i, 0)),
                  pl.BlockSpec((1, d), lambda i: (0, 0))],
        out_specs=[pl.BlockSpec((tm, d), lambda i: (i, 0)),
                   pl.BlockSpec((tm, 1), lambda i: (i, 0))],
        out_shape=[jax.ShapeDtypeStruct((n, d), BF16), jax.ShapeDtypeStruct((n, 1), F32)],
        compiler_params=_cparams(("parallel",)),
        name="rmsnorm",
    )(x, g.reshape(1, d))


W_CAST_VREGS = 256


def _mm_ws_kernel(a_ref, w_hbm, rs_ref, *rest, layer, transposed, n_plain, seg_start, relu2, n_side):
    side_refs, rest = rest[:n_side], rest[n_side:]
    o_ref, rest = rest[0], rest[1:]
    side_o_refs, (stage_ref, wb_ref, sem) = rest[:n_side], rest[n_side:]
    for src, dst in zip(side_refs, side_o_refs):
        dst[...] = src[...].astype(BF16)
    j = pl.program_id(0)
    i = pl.program_id(1)
    nj = pl.num_programs(0)
    tn = o_ref.shape[1]

    def w_copy(jj):
        if transposed:
            start = jnp.where(jj < n_plain, jj * tn, seg_start + (jj - n_plain) * tn)
            src = w_hbm.at[layer, pl.ds(pl.multiple_of(start, 16), tn), :]
        else:
            src = w_hbm.at[layer, :, pl.ds(pl.multiple_of(jj * tn, tn), tn)]
        return pltpu.make_async_copy(src, stage_ref, sem)

    @pl.when(i == 0)
    def _():
        @pl.when(j == 0)
        def _():
            w_copy(j).start()

        w_copy(j).wait()
        rows, cols = stage_ref.shape
        step = min(rows, W_CAST_VREGS * 8 * LANES // cols)
        assert rows % step == 0

        def cast(r, _):
            sl = pl.ds(pl.multiple_of(r * step, step), step)
            wb_ref[sl, :] = stage_ref[sl, :].astype(BF16)
            return 0

        lax.fori_loop(0, rows // step, cast, 0)

        @pl.when(j + 1 < nj)
        def _():
            w_copy(j + 1).start()

    if transposed:
        acc = lax.dot_general(a_ref[...], wb_ref[...], (((1,), (1,)), ((), ())), preferred_element_type=F32)
    else:
        acc = jnp.dot(a_ref[...], wb_ref[...], preferred_element_type=F32)
    acc = acc * rs_ref[...]
    if relu2:
        acc = jnp.square(jnp.maximum(acc, 0.0))
    o_ref[...] = acc.astype(o_ref.dtype)


def matmul_wcast(a, row_scale, w3, layer, *, n_out, transposed=False, n_plain=None, seg_start=0, relu2=False,
                 sides=(), tm=1024, tn=1024, name="mm"):
    m, k = a.shape
    tm = min(tm, m)
    nj, ni = n_out // tn, m // tm
    n_plain = nj if n_plain is None else n_plain
    assert transposed or n_plain == nj
    stage_shape = (tn, k) if transposed else (k, tn)
    in_specs = [pl.BlockSpec((tm, k), lambda j, i: (i, 0)),
                pl.BlockSpec(memory_space=pl.ANY),
                pl.BlockSpec((tm, 1), lambda j, i: (i, 0))]
    out_specs = [pl.BlockSpec((tm, tn), lambda j, i: (i, j))]
    out_shape = [jax.ShapeDtypeStruct((m, n_out), BF16)]
    args = [a, w3, row_scale]
    side_in, side_out = [], []
    for arr, slab in sides:
        _, r, c = arr.shape
        n_slab = r // slab
        assert n_slab * slab == r and n_slab <= nj * ni and slab % 16 == 0
        side_in.append(pl.BlockSpec((None, slab, c),
                                    lambda j, i, n_slab=n_slab: (layer, jnp.minimum(j * ni + i, n_slab - 1), 0)))
        side_out.append(pl.BlockSpec((slab, c), lambda j, i, n_slab=n_slab: (jnp.minimum(j * ni + i, n_slab - 1), 0)))
        out_shape.append(jax.ShapeDtypeStruct((r, c), BF16))
        args.append(arr)
    in_specs += side_in
    out_specs += side_out
    outs = pl.pallas_call(
        functools.partial(_mm_ws_kernel, layer=layer, transposed=transposed, n_plain=n_plain,
                          seg_start=seg_start, relu2=relu2, n_side=len(sides)),
        grid=(nj, ni),
        in_specs=in_specs,
        out_specs=out_specs,
        out_shape=out_shape,
        scratch_shapes=[pltpu.VMEM(stage_shape, F32),
                        pltpu.VMEM(stage_shape, BF16),
                        pltpu.SemaphoreType.DMA(())],
        compiler_params=_cparams(("arbitrary", "arbitrary")),
        name=name,
    )(*args)
    return outs[0] if not sides else outs


def _mm_res_kernel(a_ref, w_ref, r_ref, *rest, with_norm, nk):
    if with_norm:
        g_ref, o_ref, xg_ref, rs_ref, ssq_ref = rest
    else:
        (o_ref,) = rest
    j, k = pl.program_id(1), pl.program_id(2)
    nj = pl.num_programs(1)

    def step(first, last):
        base = r_ref if first else o_ref
        x = base[...] + jnp.dot(a_ref[...], w_ref[...], preferred_element_type=F32)
        o_ref[...] = x
        if with_norm and last:
            xg_ref[...] = (x * g_ref[...]).astype(BF16)
            part = jnp.sum(x * x, axis=-1, keepdims=True)

            @pl.when(j == 0)
            def _():
                ssq_ref[...] = part

            @pl.when(j > 0)
            def _():
                ssq_ref[...] += part

            @pl.when(j == nj - 1)
            def _():
                rs_ref[...] = lax.rsqrt(ssq_ref[...] * (1.0 / (nj * o_ref.shape[1])) + EPS)

    if nk == 1:
        step(True, True)
    else:
        pl.when(k == 0)(lambda: step(True, False))
        if nk > 2:
            pl.when((k > 0) & (k < nk - 1))(lambda: step(False, False))
        pl.when(k == nk - 1)(lambda: step(False, True))


def matmul_residual(a, w, res, *, norm_g=None, tm=1024, tn=1024, tk=4096, name="mm_res"):
    m, k = a.shape
    n = w.shape[-1]
    tm, tn, tk = min(tm, m), min(tn, n), min(tk, k)
    with_norm = norm_g is not None
    tile = pl.BlockSpec((tm, tn), lambda i, j, kk: (i, j))
    in_specs = [pl.BlockSpec((tm, tk), lambda i, j, kk: (i, kk)),
                pl.BlockSpec((tk, tn), lambda i, j, kk: (kk, j)),
                tile]
    out_specs, out_shape, scratch, args = [tile], [jax.ShapeDtypeStruct((m, n), F32)], [], [a, w, res]
    if with_norm:
        in_specs.append(pl.BlockSpec((1, tn), lambda i, j, kk: (0, j)))
        out_specs += [tile, pl.BlockSpec((tm, 1), lambda i, j, kk: (i, 0))]
        out_shape += [jax.ShapeDtypeStruct((m, n), BF16), jax.ShapeDtypeStruct((m, 1), F32)]
        scratch.append(pltpu.VMEM((tm, 1), F32))
        args.append(norm_g.reshape(1, n))
    outs = pl.pallas_call(
        functools.partial(_mm_res_kernel, with_norm=with_norm, nk=k // tk),
        grid=(m // tm, n // tn, k // tk),
        in_specs=in_specs,
        out_specs=out_specs,
        out_shape=out_shape,
        scratch_shapes=scratch,
        compiler_params=_cparams(("parallel", "arbitrary", "arbitrary")),
        name=name,
    )(*args)
    return outs if with_norm else outs[0]


def _mm_merge_kernel(a1_ref, w1_ref, a2_ref, w2_ref, g1_ref, g2_ref, o_ref):
    u1 = jnp.dot(a1_ref[...], w1_ref[...], preferred_element_type=F32)
    u2 = jnp.dot(a2_ref[...], w2_ref[...], preferred_element_type=F32)
    y = (jax.nn.sigmoid(g1_ref[...].astype(F32)) * u1
         + jax.nn.sigmoid(g2_ref[...].astype(F32)) * u2)
    o_ref[...] = y.astype(o_ref.dtype)


def merge_up_proj(o_a, w_a, o_r, w_r, proj, *, tm=1024, tn=1024):
    m, k = o_a.shape
    n = w_a.shape[1]
    tm = min(tm, m)
    ca, cr = COL_MERGE_A // tn, COL_MERGE_R // tn
    return pl.pallas_call(
        _mm_merge_kernel,
        grid=(m // tm, n // tn),
        in_specs=[pl.BlockSpec((tm, k), lambda i, j: (i, 0)),
                  pl.BlockSpec((k, tn), lambda i, j: (0, j)),
                  pl.BlockSpec((tm, k), lambda i, j: (i, 0)),
                  pl.BlockSpec((k, tn), lambda i, j: (0, j)),
                  pl.BlockSpec((tm, tn), lambda i, j: (i, ca + j)),
                  pl.BlockSpec((tm, tn), lambda i, j: (i, cr + j))],
        out_specs=pl.BlockSpec((tm, tn), lambda i, j: (i, j)),
        out_shape=jax.ShapeDtypeStruct((m, n), BF16),
        compiler_params=_cparams(("parallel", "arbitrary")),
        name="merge_up",
    )(o_a, w_a, o_r, w_r, proj, proj)


def _t5_bucket(dist):
    n = jnp.maximum(dist, 0)
    max_exact = N_BUCKETS // 2
    large = max_exact + (jnp.log(jnp.maximum(n, 1).astype(F32) / max_exact)
                         / math.log(MAX_DISTANCE / max_exact) * (N_BUCKETS - max_exact)).astype(jnp.int32)
    large = jnp.minimum(large, N_BUCKETS - 1)
    return jnp.where(n < max_exact, n, large)


PAIR_SLOTS = WINDOW // KT + 1
N_PAIR_TABLES = 2 * PAIR_SLOTS
ZERO_PAIR_TABLE = PAIR_SLOTS + 1
N_BRANCH = 3


def _bucket_maps():
    kk = jnp.arange(Q_BLOCK)[:, None]
    r = jnp.arange(Q_BLOCK)[None, :]

    def tile(delta):
        dist = delta * Q_BLOCK + r - kk
        return jnp.where((dist >= 0) & (dist < WINDOW), _t5_bucket(dist), -1)

    neg = jnp.full((Q_BLOCK, Q_BLOCK), -1)
    far = jnp.full((Q_BLOCK, Q_BLOCK), N_BUCKETS - 1)
    pairs = [(tile(0), neg), (far, tile(1)), (tile(4), far),
             (tile(1), tile(0)), (far, far), (neg, tile(4))]
    pair_map = jnp.concatenate([jnp.concatenate(p, axis=0) for p in pairs], axis=0).astype(jnp.int32)
    mrow = jnp.arange(CMP_NEAR)[:, None]
    dist_c = r - CMP_STRIDE * (mrow - CMP_PER_QB) - (CMP_LEN - 1)
    cmp_rows = jnp.where(dist_c >= 0, _t5_bucket(dist_c), -1)
    cmp_map = jnp.concatenate([cmp_rows, jnp.full((8, Q_BLOCK), N_BUCKETS - 1)], axis=0).astype(jnp.int32)
    return pair_map, cmp_map


def _bias_kernel(relb_ref, pmap_ref, cmap_ref, tp_ref, cb_ref, br_ref):
    h = pl.program_id(0)
    far = relb_ref[N_BUCKETS - 1, h]

    def lookup(bmap, offset):
        out = jnp.full(bmap.shape, NEG_INF, F32)
        for b in range(N_BUCKETS):
            out = jnp.where(bmap == b, (relb_ref[b, h] - offset) * LOG2E, out)
        return out

    tp_ref[...] = lookup(pmap_ref[...], far)
    cb_ref[...] = lookup(cmap_ref[...], 0.0)
    full = jnp.full(br_ref.shape, far * LOG2E, F32)
    hi = full.astype(BF16).astype(F32)
    row = lax.broadcasted_iota(jnp.int32, br_ref.shape, 0)
    br_ref[...] = jnp.where(row == 0, hi, jnp.where(row == 1, full - hi, 0.0))


def bias_tables(rel_bias):
    pair_map, cmp_map = _bucket_maps()
    rp, rc = pair_map.shape[0], cmp_map.shape[0]

    def out_spec(rows):
        return pl.BlockSpec((None, rows, LANES), lambda h: (h // NSA_GROUP, 0, h % NSA_GROUP))

    tp, cb, br = pl.pallas_call(
        _bias_kernel,
        grid=(NSA_HEADS,),
        in_specs=[pl.BlockSpec(memory_space=pltpu.SMEM),
                  pl.BlockSpec((rp, LANES), lambda h: (0, 0)),
                  pl.BlockSpec((rc, LANES), lambda h: (0, 0))],
        out_specs=[out_spec(rp), out_spec(rc), out_spec(BIAS_ROWS)],
        out_shape=[jax.ShapeDtypeStruct((NSA_KV_HEADS, rp, GQ), F32),
                   jax.ShapeDtypeStruct((NSA_KV_HEADS, rc, GQ), F32),
                   jax.ShapeDtypeStruct((NSA_KV_HEADS, BIAS_ROWS, GQ), F32)],
        compiler_params=_cparams(("arbitrary",)),
        name="bias_tables",
    )(rel_bias, pair_map, cmp_map)
    return tp.reshape(NSA_KV_HEADS, N_PAIR_TABLES, KT, GQ), cb, br


def _kv_prep_kernel(ks_ref, vs_ref, kw_ref, vw_ref, g_ref, kso_ref, vso_ref, kwo_ref, vwo_ref):
    p = pl.program_id(2)

    def norm(x, g):
        ms = jnp.mean(x * x, axis=-1, keepdims=True)
        return x * lax.rsqrt(ms + EPS) * g

    lane = lax.broadcasted_iota(jnp.int32, (KT, LANES), 1)
    row = lax.broadcasted_iota(jnp.int32, (KT, LANES), 0)
    ones = jnp.where(lane < 2, 1.0, 0.0)
    blk = (KT // SEL_LEN) * p + jnp.right_shift(row, SEL_LEN.bit_length() - 1)
    onehot = jnp.where(lane == blk + (SEL_ROW0 - HEAD_DIM), 1.0, 0.0)
    ks = norm(ks_ref[...].astype(F32), g_ref[1:2, :])
    kw = norm(kw_ref[...].astype(F32), g_ref[2:3, :])
    kso_ref[...] = jnp.concatenate([ks, ones + onehot], axis=1).astype(BF16)
    kwo_ref[...] = jnp.concatenate([kw, ones], axis=1).astype(BF16)
    vso_ref[...] = vs_ref[...].astype(F32).T.astype(BF16)
    vwo_ref[...] = vw_ref[...].astype(F32).T.astype(BF16)


def kv_prep(proj, k_norm_g, B, T):
    npair = T // KT
    G = NSA_KV_HEADS
    assert SEL_LEN & (SEL_LEN - 1) == 0 and SEL_ROW0 + T // SEL_LEN <= K_AUG

    def col(c0):
        return pl.BlockSpec((KT, HEAD_DIM), lambda b, g, p: (b * npair + p, c0 // HEAD_DIM + g))

    k_spec = pl.BlockSpec((None, None, KT, K_AUG), lambda b, g, p: (b, g, p, 0))
    v_spec = pl.BlockSpec((None, None, None, HEAD_DIM, KT), lambda b, g, p: (b, g, p, 0, 0))
    k_shape = jax.ShapeDtypeStruct((B, G, T, K_AUG), BF16)
    v_shape = jax.ShapeDtypeStruct((B, G, npair, HEAD_DIM, KT), BF16)
    return pl.pallas_call(
        _kv_prep_kernel,
        grid=(B, G, npair),
        in_specs=[col(COL_KSL), col(COL_VSL), col(COL_KWN), col(COL_VWN),
                  pl.BlockSpec((3, HEAD_DIM), lambda b, g, p: (0, 0))],
        out_specs=[k_spec, v_spec, k_spec, v_spec],
        out_shape=[k_shape, v_shape, k_shape, v_shape],
        compiler_params=_cparams(("parallel", "parallel", "parallel")),
        name="kv_prep",
    )(proj, proj, proj, proj, k_norm_g)


def _compress_kernel(ck_ref, cv_ref, pos_ref, w1_ref, w2_ref, g_ref, kc_ref, vct_ref):
    nchunk = ck_ref.shape[0]
    half = CMP_STRIDE * HEAD_DIM

    def mlp(x_ref, idx):
        x = x_ref[...].astype(F32)
        xa = (x + pos_ref[idx, :, 0:half]).astype(BF16)
        xb = (x + pos_ref[idx, :, half:2 * half]).astype(BF16)
        u = jnp.dot(xa, w1_ref[idx, 0:half, :], preferred_element_type=F32)
        v = jnp.dot(xb, w1_ref[idx, half:2 * half, :], preferred_element_type=F32)
        pre = u + pltpu.roll(v, nchunk - 1, 0)
        return jnp.dot(jax.nn.gelu(pre).astype(BF16), w2_ref[idx], preferred_element_type=F32)

    kc = mlp(ck_ref, 0)
    ms = jnp.mean(kc * kc, axis=-1, keepdims=True)
    kc_ref[...] = (kc * lax.rsqrt(ms + EPS) * g_ref[0:1, :]).astype(BF16)
    vct_ref[...] = mlp(cv_ref, 1).T.astype(BF16)


def compress_kv(proj, cmp_pos, cmp_w1, cmp_w2, k_norm_g, B, T):
    G = NSA_KV_HEADS
    nchunk = T // CMP_STRIDE
    half = CMP_STRIDE * HEAD_DIM
    ckv = proj[:, COL_KC:COL_KC + 2 * D_KV].reshape(B, nchunk, CMP_STRIDE, 2, G, HEAD_DIM)
    ckv = ckv.transpose(3, 0, 4, 1, 2, 5).reshape(2, B, G, nchunk, half)
    pos = cmp_pos.reshape(2, 1, CMP_LEN * HEAD_DIM)
    in_spec = pl.BlockSpec((None, None, nchunk, half), lambda b, g: (b, g, 0, 0))
    return pl.pallas_call(
        _compress_kernel,
        grid=(B, G),
        in_specs=[in_spec, in_spec,
                  pl.BlockSpec((2, 1, CMP_LEN * HEAD_DIM), lambda b, g: (0, 0, 0)),
                  pl.BlockSpec((2, CMP_LEN * HEAD_DIM, CMP_HIDDEN), lambda b, g: (0, 0, 0)),
                  pl.BlockSpec((2, CMP_HIDDEN, HEAD_DIM), lambda b, g: (0, 0, 0)),
                  pl.BlockSpec((3, HEAD_DIM), lambda b, g: (0, 0))],
        out_specs=[pl.BlockSpec((None, None, nchunk, HEAD_DIM), lambda b, g: (b, g, 0, 0)),
                   pl.BlockSpec((None, None, HEAD_DIM, nchunk), lambda b, g: (b, g, 0, 0))],
        out_shape=[jax.ShapeDtypeStruct((B, G, nchunk, HEAD_DIM), BF16),
                   jax.ShapeDtypeStruct((B, G, HEAD_DIM, nchunk), BF16)],
        compiler_params=_cparams(("parallel", "parallel")),
        name="compress_kv",
    )(ckv[0], ckv[1], pos, cmp_w1.astype(BF16), cmp_w2.astype(BF16), k_norm_g)


def _nsa_kernel(q_ref, gate_ref, kc_ref, vct_ref, ks_ref, vst_ref, kw_ref, vwt_ref,
                tp_ref, cbt_ref, brow_ref, qg_ref, ovl_ref, o_ref, acc_ref, qa_ref, cb_ref, sa_ref, sb_ref,
                *, top_n):
    i = pl.program_id(2)
    parity = jnp.bitwise_and(i, 1)
    pd = jnp.right_shift(i, 1)
    nchunk = kc_ref.shape[0]
    n_s = ovl_ref.shape[0]
    scale = HEAD_DIM ** -0.5 * LOG2E

    q = q_ref[...].astype(F32)
    qt = jnp.concatenate([q[:, h * HEAD_DIM:(h + 1) * HEAD_DIM].T for h in range(NSA_GROUP)], axis=1)
    ms = jnp.mean(qt * qt, axis=0, keepdims=True)
    q2t = (qt * lax.rsqrt(ms + EPS) * (qg_ref[...] * scale)).astype(BF16)
    qa_ref[0:HEAD_DIM, :] = q2t
    qa_ref[BIAS_ROW0:SEL_ROW0, :] = brow_ref[...].astype(BF16)
    qa_ref[SEL_ROW0:K_AUG, :] = jnp.zeros((K_AUG - SEL_ROW0, GQ), BF16)

    crow = lax.broadcasted_iota(jnp.int32, (nchunk, GQ), 0)
    near0 = CMP_PER_QB * i - CMP_PER_QB
    cb_ref[...] = jnp.where(crow < near0, cbt_ref[CMP_NEAR:CMP_NEAR + 1, :], NEG_INF)

    @pl.when(i == 0)
    def _():
        cb_ref[0:CMP_PER_QB, :] = cbt_ref[CMP_PER_QB:CMP_NEAR, :]

    @pl.when(i > 0)
    def _():
        cb_ref[pl.ds(pl.multiple_of(near0, CMP_PER_QB), CMP_NEAR), :] = cbt_ref[0:CMP_NEAR, :]

    lc = jnp.dot(kc_ref[...], q2t, preferred_element_type=F32) + cb_ref[...]
    mc = jnp.max(lc, axis=0, keepdims=True)
    ec = jnp.where(lc > 0.5 * NEG_INF, jnp.exp2(lc - mc), 0.0)
    pc = ec * (1.0 / jnp.maximum(jnp.sum(ec, axis=0, keepdims=True), 1e-30))
    o_cmp = jnp.dot(vct_ref[...], pc.astype(BF16), preferred_element_type=F32)

    pt = pc[:, 0:Q_BLOCK]
    for h in range(1, NSA_GROUP):
        pt = pt + pc[:, h * Q_BLOCK:(h + 1) * Q_BLOCK]
    ovl = ovl_ref[...]
    p1 = pt.astype(BF16)
    r1 = pt - p1.astype(F32)
    p2 = r1.astype(BF16)
    p3 = (r1 - p2.astype(F32)).astype(BF16)
    imp = (jnp.dot(ovl, p1, preferred_element_type=F32)
           + jnp.dot(ovl, p2, preferred_element_type=F32)
           + jnp.dot(ovl, p3, preferred_element_type=F32))
    sidx = lax.broadcasted_iota(jnp.int32, (n_s, Q_BLOCK), 0)
    rq = lax.broadcasted_iota(jnp.int32, (n_s, Q_BLOCK), 1)
    cur = (Q_BLOCK // SEL_LEN) * i + jnp.where(rq >= SEL_LEN, 1, 0)
    forced = (sidx == 0) | (sidx == cur) | (sidx == cur - 1)
    score = jnp.where(forced, FORCE_SCORE, jnp.where(sidx > cur, NEG_INF, imp))
    rank = jnp.zeros((n_s, Q_BLOCK), F32)
    for sp in range(n_s):
        row = score[sp:sp + 1, :]
        rank = rank + jnp.where(sidx > sp, jnp.where(row >= score, 1.0, 0.0), jnp.where(row > score, 1.0, 0.0))
    selb = jnp.where(rank < top_n, 0.0, NEG_INF)
    qa_ref[SEL_ROW0:SEL_ROW0 + n_s, :] = jnp.concatenate([selb] * NSA_GROUP, axis=1).astype(BF16)

    def flash(k_ref, vt_ref, count, pair_of, table_of):
        last = count - 1

        def logits(t):
            p = pair_of(jnp.minimum(t, last))
            k = k_ref[pl.ds(pl.multiple_of(p * KT, KT), KT), :]
            return jnp.dot(k, qa_ref[...], preferred_element_type=F32) + tp_ref[table_of(p)]

        def consume(s_ref, t, carry):
            m, l = carry
            s = s_ref[...]
            m_new = jnp.maximum(m, jnp.max(s, axis=0, keepdims=True))
            alpha = jnp.exp2(m - m_new)
            pe = jnp.exp2(s - m_new)
            l = alpha * l + jnp.sum(pe, axis=0, keepdims=True)
            acc_ref[...] = acc_ref[...] * alpha + jnp.dot(vt_ref[pair_of(t)], pe.astype(BF16),
                                                          preferred_element_type=F32)
            return m_new, l

        def two_pairs(u, carry):
            t = 2 * u
            sb_ref[...] = logits(t + 1)
            carry = consume(sa_ref, t, carry)
            sa_ref[...] = logits(t + 2)
            return consume(sb_ref, t + 1, carry)

        acc_ref[...] = jnp.zeros_like(acc_ref)
        sa_ref[...] = logits(0)
        carry = (jnp.full((1, GQ), NEG_INF, F32), jnp.zeros((1, GQ), F32))
        carry = lax.fori_loop(0, jnp.right_shift(count, 1), two_pairs, carry)
        _, l = lax.cond(jnp.bitwise_and(count, 1) == 1, lambda c: consume(sa_ref, last, c), lambda c: c, carry)
        return acc_ref[...] * (1.0 / l)

    def near_table(p):
        return PAIR_SLOTS * parity + (pd - p)

    o_sel = flash(ks_ref, vst_ref, pd + 1, lambda t: t,
                  lambda p: jnp.where(pd - p <= 1, near_table(p), ZERO_PAIR_TABLE))

    o_win = flash(kw_ref, vwt_ref, jnp.minimum(pd, PAIR_SLOTS - 1) + 1, lambda t: pd - t, near_table)

    gt = jax.nn.sigmoid(gate_ref[...].astype(F32)).T
    first_group = pl.program_id(1) == 0

    def gate_row(br):
        rows = []
        for h in range(NSA_GROUP):
            lo = N_BRANCH * h + br
            hi = N_BRANCH * (NSA_GROUP + h) + br
            rows.append(jnp.where(first_group, gt[lo:lo + 1, :], gt[hi:hi + 1, :]))
        return jnp.concatenate(rows, axis=1)

    ot = gate_row(0) * o_cmp + gate_row(1) * o_sel + gate_row(2) * o_win
    for h in range(NSA_GROUP):
        o_ref[:, h * HEAD_DIM:(h + 1) * HEAD_DIM] = ot[:, h * Q_BLOCK:(h + 1) * Q_BLOCK].T.astype(o_ref.dtype)


def _overlap_t(T):
    n_c = T // CMP_STRIDE
    n_s = T // SEL_LEN
    c_start = jnp.arange(n_c) * CMP_STRIDE
    c_end = c_start + CMP_LEN - 1
    s_start = jnp.arange(n_s) * SEL_LEN
    ov = (c_start[None, :] < s_start[:, None] + SEL_LEN) & (c_end[None, :] >= s_start[:, None])
    ov = ov & (jnp.arange(n_c)[None, :] < n_c - 1)
    return ov.astype(BF16)


def nsa_attention(proj, q_norm_g, kc, vct, ks, vst, kw, vwt, tp, cbt, brow, B, T):
    nb = T // Q_BLOCK
    npair = T // KT
    G = NSA_KV_HEADS
    nchunk = T // CMP_STRIDE
    n_s = T // SEL_LEN
    top_n = min(SEL_TOPN, n_s)
    k_spec = pl.BlockSpec((None, None, T, K_AUG), lambda b, g, i: (b, g, 0, 0))
    v_spec = pl.BlockSpec((None, None, npair, HEAD_DIM, KT), lambda b, g, i: (b, g, 0, 0, 0))
    gate_blk = COL_GATE // LANES
    return pl.pallas_call(
        functools.partial(_nsa_kernel, top_n=top_n),
        grid=(B, G, nb),
        in_specs=[pl.BlockSpec((Q_BLOCK, GQ), lambda b, g, i: (b * nb + i, g)),
                  pl.BlockSpec((Q_BLOCK, LANES), lambda b, g, i: (b * nb + i, gate_blk)),
                  pl.BlockSpec((None, None, nchunk, HEAD_DIM), lambda b, g, i: (b, g, 0, 0)),
                  pl.BlockSpec((None, None, HEAD_DIM, nchunk), lambda b, g, i: (b, g, 0, 0)),
                  k_spec, v_spec, k_spec, v_spec,
                  pl.BlockSpec((None, N_PAIR_TABLES, KT, GQ), lambda b, g, i: (g, 0, 0, 0)),
                  pl.BlockSpec((None, CMP_NEAR + 8, GQ), lambda b, g, i: (g, 0, 0)),
                  pl.BlockSpec((None, BIAS_ROWS, GQ), lambda b, g, i: (g, 0, 0)),
                  pl.BlockSpec((HEAD_DIM, 1), lambda b, g, i: (0, 0)),
                  pl.BlockSpec((n_s, nchunk), lambda b, g, i: (0, 0))],
        out_specs=pl.BlockSpec((Q_BLOCK, GQ), lambda b, g, i: (b * nb + i, g)),
        out_shape=jax.ShapeDtypeStruct((B * T, D_NSA), BF16),
        scratch_shapes=[pltpu.VMEM((HEAD_DIM, GQ), F32),
                        pltpu.VMEM((K_AUG, GQ), BF16),
                        pltpu.VMEM((nchunk, GQ), F32),
                        pltpu.VMEM((KT, GQ), F32),
                        pltpu.VMEM((KT, GQ), F32)],
        compiler_params=_cparams(("parallel", "parallel", "arbitrary")),
        name="nsa_attention",
    )(proj, proj, kc, vct, ks, vst, kw, vwt, tp, cbt, brow, q_norm_g.reshape(HEAD_DIM, 1), _overlap_t(T))


def _ret_tables(T):
    H, C = RET_HEADS, RET_CHUNK
    half = RET_HEAD_DIM // 2
    pos = jnp.arange(T, dtype=F32)
    freqs = ROPE_BASE ** (-jnp.arange(half, dtype=F32) / half)
    ang = pos[:, None] * freqs[None, :]
    log_gamma = jnp.log(1.0 - 2.0 ** (-5.0 - jnp.arange(H, dtype=F32)))
    n = jnp.arange(C, dtype=F32)
    diff = n[:, None] - n[None, :]
    decay_in = jnp.where(diff >= 0, jnp.exp(jnp.maximum(diff, 0.0)[None] * log_gamma[:, None, None]), 0.0)
    q_decay = jnp.exp((n + 1.0)[None] * log_gamma[:, None])
    k_decay = jnp.exp((C - 1.0 - n)[None] * log_gamma[:, None])
    chunk_decay = jnp.exp(C * log_gamma)
    qk_decay = jnp.stack([q_decay, k_decay], axis=1)[..., None]
    return jnp.cos(ang), jnp.sin(ang), decay_in, qk_decay, chunk_decay


def _retention_kernel(cd_ref, q_ref, k_ref, v_ref, g_ref, cos_ref, sin_ref, din_ref, qkd_ref, gn_ref,
                      o_ref, state_ref):
    c = pl.program_id(1)
    d = RET_HEAD_DIM
    half = d // 2

    @pl.when(c == 0)
    def _():
        state_ref[...] = jnp.zeros_like(state_ref)

    cos = cos_ref[...]
    sin = sin_ref[...]

    def rope(x):
        x1, x2 = x[:, :half], x[:, half:]
        return jnp.concatenate([x1 * cos - x2 * sin, x2 * cos + x1 * sin], axis=1)

    for h in range(RET_HEADS):
        sl = slice(h * d, (h + 1) * d)
        qr = rope(q_ref[:, sl].astype(F32))
        kr = rope(k_ref[:, sl].astype(F32)) * (d ** -0.5)
        v = v_ref[:, sl]
        qb = qr.astype(BF16)
        inner = lax.dot_general(qb, kr.astype(BF16), (((1,), (1,)), ((), ())), preferred_element_type=F32)
        inner = (inner * din_ref[h]).astype(BF16)
        state = state_ref[h]
        cross = jnp.dot((qr * qkd_ref[h, 0]).astype(BF16), state.astype(BF16), preferred_element_type=F32)
        o = jnp.dot(inner, v, preferred_element_type=F32) + cross
        kdt = (kr * qkd_ref[h, 1]).T.astype(BF16)
        state_ref[h] = state * cd_ref[h] + jnp.dot(kdt, v, preferred_element_type=F32)
        mu = jnp.mean(o, axis=-1, keepdims=True)
        var = jnp.mean(jnp.square(o - mu), axis=-1, keepdims=True)
        on = (o - mu) * lax.rsqrt(var + EPS) * gn_ref[:, sl]
        o_ref[:, sl] = (jax.nn.silu(g_ref[:, sl].astype(F32)) * on).astype(o_ref.dtype)


def retention(proj, gn_g, B, T):
    nc = T // RET_CHUNK
    cos, sin, decay_in, qk_decay, chunk_decay = _ret_tables(T)
    cblk = COL_RET // D_RET

    def seg(k):
        return pl.BlockSpec((RET_CHUNK, D_RET), lambda b, c: (b * nc + c, cblk + k))

    rope_spec = pl.BlockSpec((RET_CHUNK, RET_HEAD_DIM // 2), lambda b, c: (c, 0))
    return pl.pallas_call(
        _retention_kernel,
        grid=(B, nc),
        in_specs=[pl.BlockSpec(memory_space=pltpu.SMEM),
                  seg(0), seg(1), seg(2), seg(3), rope_spec, rope_spec,
                  pl.BlockSpec((RET_HEADS, RET_CHUNK, RET_CHUNK), lambda b, c: (0, 0, 0)),
                  pl.BlockSpec((RET_HEADS, 2, RET_CHUNK, 1), lambda b, c: (0, 0, 0, 0)),
                  pl.BlockSpec((1, D_RET), lambda b, c: (0, 0))],
        out_specs=pl.BlockSpec((RET_CHUNK, D_RET), lambda b, c: (b * nc + c, 0)),
        out_shape=jax.ShapeDtypeStruct((B * T, D_RET), BF16),
        scratch_shapes=[pltpu.VMEM((RET_HEADS, RET_HEAD_DIM, RET_HEAD_DIM), F32)],
        compiler_params=_cparams(("parallel", "arbitrary")),
        name="retention",
    )(chunk_decay, proj, proj, proj, proj, cos, sin, decay_in, qk_decay, gn_g.reshape(1, D_RET))


def kernel(x, norm1_g, w_in, nsa_q_norm_g, nsa_k_norm_g, cmp_pos, cmp_w1, cmp_w2, ret_gn_g,
           w_up_nsa, w_up_ret, w_out, norm2_g, w_ff1, w_ff2, rel_bias):
    B, T, D = x.shape
    depth = w_in.shape[0]
    xf = x.reshape(B * T, D)
    tp, cbt, brow = bias_tables(rel_bias)
    w_in_t = jnp.swapaxes(w_in, 1, 2)
    h, rs = rmsnorm_split(xf, norm1_g[0])
    for l in range(depth):
        proj, w_out_b, w_upa_b, w_upr_b = matmul_wcast(
            h, rs, w_in_t, l, n_out=D_PROJ, transposed=True, n_plain=COL_RET // 1024, seg_start=N_REAL_A,
            sides=((w_out, 32), (w_up_nsa, 16), (w_up_ret, 16)), name="in_proj")
        ks, vst, kw, vwt = kv_prep(proj, nsa_k_norm_g[l], B, T)
        kc, vct = compress_kv(proj, cmp_pos[l], cmp_w1[l], cmp_w2[l], nsa_k_norm_g[l], B, T)
        o_a = nsa_attention(proj, nsa_q_norm_g[l], kc, vct, ks, vst, kw, vwt, tp, cbt, brow, B, T)
        o_r = retention(proj, ret_gn_g[l], B, T)
        y = merge_up_proj(o_a, w_upa_b, o_r, w_upr_b, proj)
        xf, h2, rs2 = matmul_residual(y, w_out_b, xf, norm_g=norm2_g[l], name="out_proj")
        a, w_ff2_b = matmul_wcast(h2, rs2, w_ff1, l, n_out=D_FF, relu2=True, sides=((w_ff2, 128),), name="ffn_up")
        if l + 1 < depth:
            xf, h, rs = matmul_residual(a, w_ff2_b, xf, norm_g=norm1_g[l + 1], name="ffn_down")
        else:
            xf = matmul_residual(a, w_ff2_b, xf, tk=4096, name="ffn_down")
    return xf.reshape(B, T, D)
```

```python
import functools
import math

import jax
import jax.numpy as jnp
from jax import lax
from jax.experimental import pallas as pl
from jax.experimental.pallas import tpu as pltpu

F32 = jnp.float32
BF16 = jnp.bfloat16

D_MODEL = 4096
HEAD_DIM = 128
D_NSA = D_MODEL // 2
NSA_HEADS = D_NSA // HEAD_DIM
NSA_KV_HEADS = 2
NSA_GROUP = NSA_HEADS // NSA_KV_HEADS
D_KV = NSA_KV_HEADS * HEAD_DIM
CMP_LEN = 32
CMP_STRIDE = 16
CMP_HIDDEN = 256
SEL_LEN = 64
SEL_TOPN = 16
WINDOW = 512
Q_BLOCK = 128
D_RET = D_MODEL // 2
RET_HEADS = 8
RET_HEAD_DIM = D_RET // RET_HEADS
RET_CHUNK = 128
D_FF = 4 * D_MODEL
N_BUCKETS = 32
MAX_DISTANCE = 128
ROPE_BASE = 10000.0
EPS = 1e-6
NEG_INF = -1e30
FORCE_SCORE = 1e9
LOG2E = 1.4426950408889634

COL_Q = 0
COL_KC = D_NSA
COL_KSL = COL_KC + 2 * D_KV
COL_VSL = COL_KSL + D_KV
COL_KWN = COL_VSL + D_KV
COL_VWN = COL_KWN + D_KV
COL_GATE = COL_VWN + D_KV
N_BRANCH = 3
N_GATE = N_BRANCH * NSA_HEADS
COL_RET = 4096
COL_MERGE_A = COL_RET + 4 * D_RET
COL_MERGE_R = COL_MERGE_A + D_MODEL
D_PROJ = COL_MERGE_R + D_MODEL
N_REAL_A = D_NSA + 6 * D_KV + N_GATE

LANES = 128
GQ = NSA_GROUP * Q_BLOCK
KT = 2 * Q_BLOCK
K_AUG = HEAD_DIM + LANES
BIAS_ROW0 = HEAD_DIM
BIAS_ROWS = 16
SEL_ROW0 = BIAS_ROW0 + BIAS_ROWS
CMP_PER_QB = Q_BLOCK // CMP_STRIDE
CMP_NEAR = 2 * CMP_PER_QB
PAIR_SLOTS = WINDOW // KT + 1
N_PAIR_TABLES = 2 * PAIR_SLOTS
ZERO_PAIR_TABLE = PAIR_SLOTS + 1
VMEM_LIMIT = 63 * 1024 * 1024


def _cparams(sem, vmem=VMEM_LIMIT):
    return pltpu.CompilerParams(dimension_semantics=sem, vmem_limit_bytes=vmem)


def _rmsnorm_kernel(x_ref, g_ref, xg_ref, rs_ref):
    x = x_ref[...]
    xg_ref[...] = (x * g_ref[...]).astype(BF16)
    rs_ref[...] = lax.rsqrt(jnp.mean(x * x, axis=-1, keepdims=True) + EPS)


def rmsnorm_split(x, g, tm=256):
    n, d = x.shape
    return pl.pallas_call(
        _rmsnorm_kernel,
        grid=(n // tm,),
        in_specs=[pl.BlockSpec((tm, d), lambda i: (i, 0)),
                  pl.BlockSpec((1, d), lambda i: (0, 0))],
        out_specs=[pl.BlockSpec((tm, d), lambda i: (i, 0)),
                   pl.BlockSpec((tm, 1), lambda i: (i, 0))],
        out_shape=[jax.ShapeDtypeStruct((n, d), BF16), jax.ShapeDtypeStruct((n, 1), F32)],
        compiler_params=_cparams(("parallel",)),
        name="rmsnorm",
    )(x, g.reshape(1, d))


W_CAST_VREGS = 256


def _mm_ws_kernel(a_ref, w_hbm, rs_ref, *rest, layer, transposed, n_plain, seg_start, relu2, n_side):
    side_refs, rest = rest[:n_side], rest[n_side:]
    o_ref, rest = rest[0], rest[1:]
    side_o_refs, (stage_ref, wb_ref, sem) = rest[:n_side], rest[n_side:]
    for src, dst in zip(side_refs, side_o_refs):
        dst[...] = src[...].astype(BF16)
    j = pl.program_id(0)
    i = pl.program_id(1)
    nj = pl.num_programs(0)
    tn = o_ref.shape[1]

    def w_copy(jj):
        if transposed:
            start = jnp.where(jj < n_plain, jj * tn, seg_start + (jj - n_plain) * tn)
            src = w_hbm.at[layer, pl.ds(pl.multiple_of(start, 16), tn), :]
        else:
            src = w_hbm.at[layer, :, pl.ds(pl.multiple_of(jj * tn, tn), tn)]
        return pltpu.make_async_copy(src, stage_ref, sem)

    @pl.when(i == 0)
    def _():
        @pl.when(j == 0)
        def _():
            w_copy(j).start()

        w_copy(j).wait()
        rows, cols = stage_ref.shape
        step = min(rows, W_CAST_VREGS * 8 * LANES // cols)
        assert rows % step == 0

        def cast(r, _):
            sl = pl.ds(pl.multiple_of(r * step, step), step)
            wb_ref[sl, :] = stage_ref[sl, :].astype(BF16)
            return 0

        lax.fori_loop(0, rows // step, cast, 0)

        @pl.when(j + 1 < nj)
        def _():
            w_copy(j + 1).start()

    if transposed:
        acc = lax.dot_general(a_ref[...], wb_ref[...], (((1,), (1,)), ((), ())), preferred_element_type=F32)
    else:
        acc = jnp.dot(a_ref[...], wb_ref[...], preferred_element_type=F32)
    acc = acc * rs_ref[...]
    if relu2:
        acc = jnp.square(jnp.maximum(acc, 0.0))
    o_ref[...] = acc.astype(o_ref.dtype)


def matmul_wcast(a, row_scale, w3, layer, *, n_out, transposed=False, n_plain=None, seg_start=0, relu2=False,
                 sides=(), tm=1024, tn=1024, name="mm"):
    m, k = a.shape
    tm = min(tm, m)
    nj, ni = n_out // tn, m // tm
    n_plain = nj if n_plain is None else n_plain
    assert transposed or n_plain == nj
    stage_shape = (tn, k) if transposed else (k, tn)
    in_specs = [pl.BlockSpec((tm, k), lambda j, i: (i, 0)),
                pl.BlockSpec(memory_space=pl.ANY),
                pl.BlockSpec((tm, 1), lambda j, i: (i, 0))]
    out_specs = [pl.BlockSpec((tm, tn), lambda j, i: (i, j))]
    out_shape = [jax.ShapeDtypeStruct((m, n_out), BF16)]
    args = [a, w3, row_scale]
    side_in, side_out = [], []
    for arr, slab in sides:
        _, r, c = arr.shape
        n_slab = r // slab
        assert n_slab * slab == r and n_slab <= nj * ni and slab % 16 == 0
        side_in.append(pl.BlockSpec((None, slab, c),
                                    lambda j, i, n_slab=n_slab: (layer, jnp.minimum(j * ni + i, n_slab - 1), 0)))
        side_out.append(pl.BlockSpec((slab, c), lambda j, i, n_slab=n_slab: (jnp.minimum(j * ni + i, n_slab - 1), 0)))
        out_shape.append(jax.ShapeDtypeStruct((r, c), BF16))
        args.append(arr)
    in_specs += side_in
    out_specs += side_out
    outs = pl.pallas_call(
        functools.partial(_mm_ws_kernel, layer=layer, transposed=transposed, n_plain=n_plain,
                          seg_start=seg_start, relu2=relu2, n_side=len(sides)),
        grid=(nj, ni),
        in_specs=in_specs,
        out_specs=out_specs,
        out_shape=out_shape,
        scratch_shapes=[pltpu.VMEM(stage_shape, F32),
                        pltpu.VMEM(stage_shape, BF16),
                        pltpu.SemaphoreType.DMA(())],
        compiler_params=_cparams(("arbitrary", "arbitrary")),
        name=name,
    )(*args)
    return outs[0] if not sides else outs


def _mm_res_kernel(a_ref, w_ref, r_ref, *rest, with_norm, nk):
    if with_norm:
        g_ref, o_ref, xg_ref, rs_ref, ssq_ref = rest
    else:
        (o_ref,) = rest
    j, k = pl.program_id(1), pl.program_id(2)
    nj = pl.num_programs(1)

    def step(first, last):
        base = r_ref if first else o_ref
        x = base[...] + jnp.dot(a_ref[...], w_ref[...], preferred_element_type=F32)
        o_ref[...] = x
        if with_norm and last:
            xg_ref[...] = (x * g_ref[...]).astype(BF16)
            part = jnp.sum(x * x, axis=-1, keepdims=True)

            @pl.when(j == 0)
            def _():
                ssq_ref[...] = part

            @pl.when(j > 0)
            def _():
                ssq_ref[...] += part

            @pl.when(j == nj - 1)
            def _():
                rs_ref[...] = lax.rsqrt(ssq_ref[...] * (1.0 / (nj * o_ref.shape[1])) + EPS)

    if nk == 1:
        step(True, True)
    else:
        pl.when(k == 0)(lambda: step(True, False))
        if nk > 2:
            pl.when((k > 0) & (k < nk - 1))(lambda: step(False, False))
        pl.when(k == nk - 1)(lambda: step(False, True))


def matmul_residual(a, w, res, *, norm_g=None, tm=1024, tn=1024, tk=4096, name="mm_res"):
    m, k = a.shape
    n = w.shape[-1]
    tm, tn, tk = min(tm, m), min(tn, n), min(tk, k)
    with_norm = norm_g is not None
    tile = pl.BlockSpec((tm, tn), lambda i, j, kk: (i, j))
    in_specs = [pl.BlockSpec((tm, tk), lambda i, j, kk: (i, kk)),
                pl.BlockSpec((tk, tn), lambda i, j, kk: (kk, j)),
                tile]
    out_specs, out_shape, scratch, args = [tile], [jax.ShapeDtypeStruct((m, n), F32)], [], [a, w, res]
    if with_norm:
        in_specs.append(pl.BlockSpec((1, tn), lambda i, j, kk: (0, j)))
        out_specs += [tile, pl.BlockSpec((tm, 1), lambda i, j, kk: (i, 0))]
        out_shape += [jax.ShapeDtypeStruct((m, n), BF16), jax.ShapeDtypeStruct((m, 1), F32)]
        scratch.append(pltpu.VMEM((tm, 1), F32))
        args.append(norm_g.reshape(1, n))
    outs = pl.pallas_call(
        functools.partial(_mm_res_kernel, with_norm=with_norm, nk=k // tk),
        grid=(m // tm, n // tn, k // tk),
        in_specs=in_specs,
        out_specs=out_specs,
        out_shape=out_shape,
        scratch_shapes=scratch,
        compiler_params=_cparams(("parallel", "arbitrary", "arbitrary")),
        name=name,
    )(*args)
    return outs if with_norm else outs[0]


def _mm_merge_kernel(a1_ref, w1_ref, a2_ref, w2_ref, g1_ref, g2_ref, o_ref):
    u1 = jnp.dot(a1_ref[...], w1_ref[...], preferred_element_type=F32)
    u2 = jnp.dot(a2_ref[...], w2_ref[...], preferred_element_type=F32)
    y = (jax.nn.sigmoid(g1_ref[...].astype(F32)) * u1
         + jax.nn.sigmoid(g2_ref[...].astype(F32)) * u2)
    o_ref[...] = y.astype(o_ref.dtype)


def merge_up_proj(o_a, w_a, o_r, w_r, proj, *, tm=1024, tn=1024):
    m, k = o_a.shape
    n = w_a.shape[1]
    tm = min(tm, m)
    ca, cr = COL_MERGE_A // tn, COL_MERGE_R // tn
    return pl.pallas_call(
        _mm_merge_kernel,
        grid=(m // tm, n // tn),
        in_specs=[pl.BlockSpec((tm, k), lambda i, j: (i, 0)),
                  pl.BlockSpec((k, tn), lambda i, j: (0, j)),
                  pl.BlockSpec((tm, k), lambda i, j: (i, 0)),
                  pl.BlockSpec((k, tn), lambda i, j: (0, j)),
                  pl.BlockSpec((tm, tn), lambda i, j: (i, ca + j)),
                  pl.BlockSpec((tm, tn), lambda i, j: (i, cr + j))],
        out_specs=pl.BlockSpec((tm, tn), lambda i, j: (i, j)),
        out_shape=jax.ShapeDtypeStruct((m, n), BF16),
        compiler_params=_cparams(("parallel", "arbitrary")),
        name="merge_up",
    )(o_a, w_a, o_r, w_r, proj, proj)


def _t5_bucket(dist):
    n = jnp.maximum(dist, 0)
    max_exact = N_BUCKETS // 2
    large = max_exact + (jnp.log(jnp.maximum(n, 1).astype(F32) / max_exact)
                         / math.log(MAX_DISTANCE / max_exact) * (N_BUCKETS - max_exact)).astype(jnp.int32)
    large = jnp.minimum(large, N_BUCKETS - 1)
    return jnp.where(n < max_exact, n, large)


def _bucket_maps():
    kk = jnp.arange(Q_BLOCK)[:, None]
    r = jnp.arange(Q_BLOCK)[None, :]

    def tile(delta):
        dist = delta * Q_BLOCK + r - kk
        return jnp.where((dist >= 0) & (dist < WINDOW), _t5_bucket(dist), -1)

    neg = jnp.full((Q_BLOCK, Q_BLOCK), -1)
    far = jnp.full((Q_BLOCK, Q_BLOCK), N_BUCKETS - 1)
    pairs = [(tile(0), neg), (far, tile(1)), (tile(4), far),
             (tile(1), tile(0)), (far, far), (neg, tile(4))]
    pair_map = jnp.concatenate([jnp.concatenate(p, axis=0) for p in pairs], axis=0).astype(jnp.int32)
    mrow = jnp.arange(CMP_NEAR)[:, None]
    dist_c = r - CMP_STRIDE * (mrow - CMP_PER_QB) - (CMP_LEN - 1)
    cmp_rows = jnp.where(dist_c >= 0, _t5_bucket(dist_c), -1)
    cmp_map = jnp.concatenate([cmp_rows, jnp.full((8, Q_BLOCK), N_BUCKETS - 1)], axis=0).astype(jnp.int32)
    return pair_map, cmp_map


def _bias_kernel(relb_ref, pmap_ref, cmap_ref, tp_ref, cb_ref, br_ref):
    h = pl.program_id(0)
    far = relb_ref[N_BUCKETS - 1, h]

    def lookup(bmap, offset):
        out = jnp.full(bmap.shape, NEG_INF, F32)
        for b in range(N_BUCKETS):
            out = jnp.where(bmap == b, (relb_ref[b, h] - offset) * LOG2E, out)
        return out

    tp_ref[...] = lookup(pmap_ref[...], far)
    cb_ref[...] = lookup(cmap_ref[...], 0.0)
    full = jnp.full(br_ref.shape, far * LOG2E, F32)
    hi = full.astype(BF16).astype(F32)
    row = lax.broadcasted_iota(jnp.int32, br_ref.shape, 0)
    br_ref[...] = jnp.where(row == 0, hi, jnp.where(row == 1, full - hi, 0.0))


def bias_tables(rel_bias):
    pair_map, cmp_map = _bucket_maps()
    rp, rc = pair_map.shape[0], cmp_map.shape[0]

    def out_spec(rows):
        return pl.BlockSpec((None, rows, LANES), lambda h: (h // NSA_GROUP, 0, h % NSA_GROUP))

    tp, cb, br = pl.pallas_call(
        _bias_kernel,
        grid=(NSA_HEADS,),
        in_specs=[pl.BlockSpec(memory_space=pltpu.SMEM),
                  pl.BlockSpec((rp, LANES), lambda h: (0, 0)),
                  pl.BlockSpec((rc, LANES), lambda h: (0, 0))],
        out_specs=[out_spec(rp), out_spec(rc), out_spec(BIAS_ROWS)],
        out_shape=[jax.ShapeDtypeStruct((NSA_KV_HEADS, rp, GQ), F32),
                   jax.ShapeDtypeStruct((NSA_KV_HEADS, rc, GQ), F32),
                   jax.ShapeDtypeStruct((NSA_KV_HEADS, BIAS_ROWS, GQ), F32)],
        compiler_params=_cparams(("arbitrary",)),
        name="bias_tables",
    )(rel_bias, pair_map, cmp_map)
    return tp.reshape(NSA_KV_HEADS, N_PAIR_TABLES, KT, GQ), cb, br


def _kv_prep_kernel(ks_ref, vs_ref, kw_ref, vw_ref, g_ref, kso_ref, vso_ref, kwo_ref, vwo_ref):
    p = pl.program_id(2)

    def norm(x, g):
        ms = jnp.mean(x * x, axis=-1, keepdims=True)
        return x * lax.rsqrt(ms + EPS) * g

    lane = lax.broadcasted_iota(jnp.int32, (KT, LANES), 1)
    row = lax.broadcasted_iota(jnp.int32, (KT, LANES), 0)
    ones = jnp.where(lane < 2, 1.0, 0.0)
    blk = (KT // SEL_LEN) * p + jnp.right_shift(row, SEL_LEN.bit_length() - 1)
    onehot = jnp.where(lane == blk + (SEL_ROW0 - HEAD_DIM), 1.0, 0.0)
    ks = norm(ks_ref[...].astype(F32), g_ref[1:2, :])
    kw = norm(kw_ref[...].astype(F32), g_ref[2:3, :])
    kso_ref[...] = jnp.concatenate([ks, ones + onehot], axis=1).astype(BF16)
    kwo_ref[...] = jnp.concatenate([kw, ones], axis=1).astype(BF16)
    vso_ref[...] = vs_ref[...].astype(F32).T.astype(BF16)
    vwo_ref[...] = vw_ref[...].astype(F32).T.astype(BF16)


def kv_prep(proj, k_norm_g, B, T):
    npair = T // KT
    G = NSA_KV_HEADS
    assert SEL_LEN & (SEL_LEN - 1) == 0 and SEL_ROW0 + T // SEL_LEN <= K_AUG

    def col(c0):
        return pl.BlockSpec((KT, HEAD_DIM), lambda b, g, p: (b * npair + p, c0 // HEAD_DIM + g))

    k_spec = pl.BlockSpec((None, None, KT, K_AUG), lambda b, g, p: (b, g, p, 0))
    v_spec = pl.BlockSpec((None, None, None, HEAD_DIM, KT), lambda b, g, p: (b, g, p, 0, 0))
    k_shape = jax.ShapeDtypeStruct((B, G, T, K_AUG), BF16)
    v_shape = jax.ShapeDtypeStruct((B, G, npair, HEAD_DIM, KT), BF16)
    return pl.pallas_call(
        _kv_prep_kernel,
        grid=(B, G, npair),
        in_specs=[col(COL_KSL), col(COL_VSL), col(COL_KWN), col(COL_VWN),
                  pl.BlockSpec((3, HEAD_DIM), lambda b, g, p: (0, 0))],
        out_specs=[k_spec, v_spec, k_spec, v_spec],
        out_shape=[k_shape, v_shape, k_shape, v_shape],
        compiler_params=_cparams(("parallel", "parallel", "parallel")),
        name="kv_prep",
    )(proj, proj, proj, proj, k_norm_g)


def _compress_kernel(ck_ref, cv_ref, pos_ref, w1_ref, w2_ref, g_ref, kc_ref, vct_ref):
    nchunk = ck_ref.shape[0]
    half = CMP_STRIDE * HEAD_DIM

    def mlp(x_ref, idx):
        x = x_ref[...].astype(F32)
        xa = (x + pos_ref[idx, :, 0:half]).astype(BF16)
        xb = (x + pos_ref[idx, :, half:2 * half]).astype(BF16)
        u = jnp.dot(xa, w1_ref[idx, 0:half, :], preferred_element_type=F32)
        v = jnp.dot(xb, w1_ref[idx, half:2 * half, :], preferred_element_type=F32)
        pre = u + pltpu.roll(v, nchunk - 1, 0)
        return jnp.dot(jax.nn.gelu(pre).astype(BF16), w2_ref[idx], preferred_element_type=F32)

    kc = mlp(ck_ref, 0)
    ms = jnp.mean(kc * kc, axis=-1, keepdims=True)
    kc_ref[...] = (kc * lax.rsqrt(ms + EPS) * g_ref[0:1, :]).astype(BF16)
    vct_ref[...] = mlp(cv_ref, 1).T.astype(BF16)


def compress_kv(proj, cmp_pos, cmp_w1, cmp_w2, k_norm_g, B, T):
    G = NSA_KV_HEADS
    nchunk = T // CMP_STRIDE
    half = CMP_STRIDE * HEAD_DIM
    ckv = proj[:, COL_KC:COL_KC + 2 * D_KV].reshape(B, nchunk, CMP_STRIDE, 2, G, HEAD_DIM)
    ckv = ckv.transpose(3, 0, 4, 1, 2, 5).reshape(2, B, G, nchunk, half)
    pos = cmp_pos.reshape(2, 1, CMP_LEN * HEAD_DIM)
    in_spec = pl.BlockSpec((None, None, nchunk, half), lambda b, g: (b, g, 0, 0))
    return pl.pallas_call(
        _compress_kernel,
        grid=(B, G),
        in_specs=[in_spec, in_spec,
                  pl.BlockSpec((2, 1, CMP_LEN * HEAD_DIM), lambda b, g: (0, 0, 0)),
                  pl.BlockSpec((2, CMP_LEN * HEAD_DIM, CMP_HIDDEN), lambda b, g: (0, 0, 0)),
                  pl.BlockSpec((2, CMP_HIDDEN, HEAD_DIM), lambda b, g: (0, 0, 0)),
                  pl.BlockSpec((3, HEAD_DIM), lambda b, g: (0, 0))],
        out_specs=[pl.BlockSpec((None, None, nchunk, HEAD_DIM), lambda b, g: (b, g, 0, 0)),
                   pl.BlockSpec((None, None, HEAD_DIM, nchunk), lambda b, g: (b, g, 0, 0))],
        out_shape=[jax.ShapeDtypeStruct((B, G, nchunk, HEAD_DIM), BF16),
                   jax.ShapeDtypeStruct((B, G, HEAD_DIM, nchunk), BF16)],
        compiler_params=_cparams(("parallel", "parallel")),
        name="compress_kv",
    )(ckv[0], ckv[1], pos, cmp_w1.astype(BF16), cmp_w2.astype(BF16), k_norm_g)


def _nsa_kernel(q_ref, gate_ref, kc_ref, vct_ref, ks_ref, vst_ref, kw_ref, vwt_ref,
                tp_ref, cbt_ref, brow_ref, qg_ref, ovl_ref, o_ref, acc_ref, qa_ref, cb_ref, sa_ref, sb_ref,
                ow_ref, *, top_n):
    i = pl.program_id(2)
    parity = jnp.bitwise_and(i, 1)
    pd = jnp.right_shift(i, 1)
    nchunk = kc_ref.shape[0]
    n_s = ovl_ref.shape[0]
    scale = HEAD_DIM ** -0.5 * LOG2E

    q = q_ref[...].astype(F32)
    qt = jnp.concatenate([q[:, h * HEAD_DIM:(h + 1) * HEAD_DIM].T for h in range(NSA_GROUP)], axis=1)
    ms = jnp.mean(qt * qt, axis=0, keepdims=True)
    q2t = (qt * lax.rsqrt(ms + EPS) * (qg_ref[...] * scale)).astype(BF16)
    qa_ref[0:HEAD_DIM, :] = q2t
    qa_ref[BIAS_ROW0:SEL_ROW0, :] = brow_ref[...].astype(BF16)
    qa_ref[SEL_ROW0:K_AUG, :] = jnp.zeros((K_AUG - SEL_ROW0, GQ), BF16)

    def win_pair(j):
        return jnp.maximum(pd - j, 0)

    def win_logits(j):
        k = kw_ref[pl.ds(pl.multiple_of(win_pair(j) * KT, KT), KT), :]
        s = jnp.dot(k, qa_ref[...], preferred_element_type=F32) + tp_ref[PAIR_SLOTS * parity + j]
        return s if j == 0 else s + jnp.where(pd >= j, 0.0, NEG_INF)

    sw = [win_logits(j) for j in range(PAIR_SLOTS)]
    mw = functools.reduce(jnp.maximum, [jnp.max(s, axis=0, keepdims=True) for s in sw])
    pw = [jnp.exp2(s - mw) for s in sw]
    lw = sum(jnp.sum(p, axis=0, keepdims=True) for p in pw)
    ow = sum(jnp.dot(vwt_ref[win_pair(j)], pw[j].astype(BF16), preferred_element_type=F32)
             for j in range(PAIR_SLOTS))
    ow_ref[...] = ow * (1.0 / lw)

    crow = lax.broadcasted_iota(jnp.int32, (nchunk, GQ), 0)
    near0 = CMP_PER_QB * i - CMP_PER_QB
    cb_ref[...] = jnp.where(crow < near0, cbt_ref[CMP_NEAR:CMP_NEAR + 1, :], NEG_INF)

    @pl.when(i == 0)
    def _():
        cb_ref[0:CMP_PER_QB, :] = cbt_ref[CMP_PER_QB:CMP_NEAR, :]

    @pl.when(i > 0)
    def _():
        cb_ref[pl.ds(pl.multiple_of(near0, CMP_PER_QB), CMP_NEAR), :] = cbt_ref[0:CMP_NEAR, :]

    lc = jnp.dot(kc_ref[...], q2t, preferred_element_type=F32) + cb_ref[...]
    mc = jnp.max(lc, axis=0, keepdims=True)
    ec = jnp.exp2(lc - mc)
    inv = jnp.where(mc > 0.5 * NEG_INF, 1.0 / jnp.maximum(jnp.sum(ec, axis=0, keepdims=True), 1e-30), 0.0)
    pc = ec * inv
    o_cmp = jnp.dot(vct_ref[...], pc.astype(BF16), preferred_element_type=F32)

    pt = pc[:, 0:Q_BLOCK]
    for h in range(1, NSA_GROUP):
        pt = pt + pc[:, h * Q_BLOCK:(h + 1) * Q_BLOCK]
    ovl = ovl_ref[...]
    p1 = pt.astype(BF16)
    r1 = pt - p1.astype(F32)
    p2 = r1.astype(BF16)
    p3 = (r1 - p2.astype(F32)).astype(BF16)
    imp = (jnp.dot(ovl, p1, preferred_element_type=F32)
           + jnp.dot(ovl, p2, preferred_element_type=F32)
           + jnp.dot(ovl, p3, preferred_element_type=F32))
    sidx = lax.broadcasted_iota(jnp.int32, (n_s, Q_BLOCK), 0)
    rq = lax.broadcasted_iota(jnp.int32, (n_s, Q_BLOCK), 1)
    cur = (Q_BLOCK // SEL_LEN) * i + jnp.where(rq >= SEL_LEN, 1, 0)
    forced = (sidx == 0) | (sidx == cur) | (sidx == cur - 1)
    score = jnp.where(forced, FORCE_SCORE, jnp.where(sidx > cur, NEG_INF, imp))
    rank = jnp.zeros((n_s, Q_BLOCK), F32)
    for sp in range(n_s):
        row = score[sp:sp + 1, :]
        rank = rank + jnp.where(sidx > sp, jnp.where(row >= score, 1.0, 0.0), jnp.where(row > score, 1.0, 0.0))
    selb = jnp.where(rank < top_n, 0.0, NEG_INF)
    qa_ref[SEL_ROW0:SEL_ROW0 + n_s, :] = jnp.concatenate([selb] * NSA_GROUP, axis=1).astype(BF16)

    def flash(k_ref, vt_ref, count, pair_of, table_of):
        last = count - 1

        def logits(t):
            p = pair_of(jnp.minimum(t, last))
            k = k_ref[pl.ds(pl.multiple_of(p * KT, KT), KT), :]
            return jnp.dot(k, qa_ref[...], preferred_element_type=F32) + tp_ref[table_of(p)]

        def consume(s_ref, t, carry):
            m, l = carry
            s = s_ref[...]
            m_new = jnp.maximum(m, jnp.max(s, axis=0, keepdims=True))
            alpha = jnp.exp2(m - m_new)
            pe = jnp.exp2(s - m_new)
            l = alpha * l + jnp.sum(pe, axis=0, keepdims=True)
            acc_ref[...] = acc_ref[...] * alpha + jnp.dot(vt_ref[pair_of(t)], pe.astype(BF16),
                                                          preferred_element_type=F32)
            return m_new, l

        def two_pairs(u, carry):
            t = 2 * u
            sb_ref[...] = logits(t + 1)
            carry = consume(sa_ref, t, carry)
            sa_ref[...] = logits(t + 2)
            return consume(sb_ref, t + 1, carry)

        acc_ref[...] = jnp.zeros_like(acc_ref)
        sa_ref[...] = logits(0)
        carry = (jnp.full((1, GQ), NEG_INF, F32), jnp.zeros((1, GQ), F32))
        carry = lax.fori_loop(0, jnp.right_shift(count, 1), two_pairs, carry)
        _, l = lax.cond(jnp.bitwise_and(count, 1) == 1, lambda c: consume(sa_ref, last, c), lambda c: c, carry)
        return acc_ref[...] * (1.0 / l)

    def near_table(p):
        return PAIR_SLOTS * parity + (pd - p)

    o_sel = flash(ks_ref, vst_ref, pd + 1, lambda t: t,
                  lambda p: jnp.where(pd - p <= 1, near_table(p), ZERO_PAIR_TABLE))

    o_win = ow_ref[...]

    gt = jax.nn.sigmoid(gate_ref[...].astype(F32)).T
    first_group = pl.program_id(1) == 0

    def gate_row(br):
        rows = []
        for h in range(NSA_GROUP):
            lo = N_BRANCH * h + br
            hi = N_BRANCH * (NSA_GROUP + h) + br
            rows.append(jnp.where(first_group, gt[lo:lo + 1, :], gt[hi:hi + 1, :]))
        return jnp.concatenate(rows, axis=1)

    ot = gate_row(0) * o_cmp + gate_row(1) * o_sel + gate_row(2) * o_win
    for h in range(NSA_GROUP):
        o_ref[:, h * HEAD_DIM:(h + 1) * HEAD_DIM] = ot[:, h * Q_BLOCK:(h + 1) * Q_BLOCK].T.astype(o_ref.dtype)


def _overlap_t(T):
    n_c = T // CMP_STRIDE
    n_s = T // SEL_LEN
    c_start = jnp.arange(n_c) * CMP_STRIDE
    c_end = c_start + CMP_LEN - 1
    s_start = jnp.arange(n_s) * SEL_LEN
    ov = (c_start[None, :] < s_start[:, None] + SEL_LEN) & (c_end[None, :] >= s_start[:, None])
    ov = ov & (jnp.arange(n_c)[None, :] < n_c - 1)
    return ov.astype(BF16)


def nsa_attention(proj, q_norm_g, kc, vct, ks, vst, kw, vwt, tp, cbt, brow, B, T):
    nb = T // Q_BLOCK
    npair = T // KT
    G = NSA_KV_HEADS
    nchunk = T // CMP_STRIDE
    n_s = T // SEL_LEN
    top_n = min(SEL_TOPN, n_s)
    k_spec = pl.BlockSpec((None, None, T, K_AUG), lambda b, g, i: (b, g, 0, 0))
    v_spec = pl.BlockSpec((None, None, npair, HEAD_DIM, KT), lambda b, g, i: (b, g, 0, 0, 0))
    gate_blk = COL_GATE // LANES
    return pl.pallas_call(
        functools.partial(_nsa_kernel, top_n=top_n),
        grid=(B, G, nb),
        in_specs=[pl.BlockSpec((Q_BLOCK, GQ), lambda b, g, i: (b * nb + i, g)),
                  pl.BlockSpec((Q_BLOCK, LANES), lambda b, g, i: (b * nb + i, gate_blk)),
                  pl.BlockSpec((None, None, nchunk, HEAD_DIM), lambda b, g, i: (b, g, 0, 0)),
                  pl.BlockSpec((None, None, HEAD_DIM, nchunk), lambda b, g, i: (b, g, 0, 0)),
                  k_spec, v_spec, k_spec, v_spec,
                  pl.BlockSpec((None, N_PAIR_TABLES, KT, GQ), lambda b, g, i: (g, 0, 0, 0)),
                  pl.BlockSpec((None, CMP_NEAR + 8, GQ), lambda b, g, i: (g, 0, 0)),
                  pl.BlockSpec((None, BIAS_ROWS, GQ), lambda b, g, i: (g, 0, 0)),
                  pl.BlockSpec((HEAD_DIM, 1), lambda b, g, i: (0, 0)),
                  pl.BlockSpec((n_s, nchunk), lambda b, g, i: (0, 0))],
        out_specs=pl.BlockSpec((Q_BLOCK, GQ), lambda b, g, i: (b * nb + i, g)),
        out_shape=jax.ShapeDtypeStruct((B * T, D_NSA), BF16),
        scratch_shapes=[pltpu.VMEM((HEAD_DIM, GQ), F32),
                        pltpu.VMEM((K_AUG, GQ), BF16),
                        pltpu.VMEM((nchunk, GQ), F32),
                        pltpu.VMEM((KT, GQ), F32),
                        pltpu.VMEM((KT, GQ), F32),
                        pltpu.VMEM((HEAD_DIM, GQ), F32)],
        compiler_params=_cparams(("parallel", "parallel", "arbitrary")),
        name="nsa_attention",
    )(proj, proj, kc, vct, ks, vst, kw, vwt, tp, cbt, brow, q_norm_g.reshape(HEAD_DIM, 1), _overlap_t(T))


def _ret_tables(T):
    H, C = RET_HEADS, RET_CHUNK
    half = RET_HEAD_DIM // 2
    pos = jnp.arange(T, dtype=F32)
    freqs = ROPE_BASE ** (-jnp.arange(half, dtype=F32) / half)
    ang = pos[:, None] * freqs[None, :]
    log_gamma = jnp.log(1.0 - 2.0 ** (-5.0 - jnp.arange(H, dtype=F32)))
    n = jnp.arange(C, dtype=F32)
    diff = n[:, None] - n[None, :]
    decay_in = jnp.where(diff >= 0, jnp.exp(jnp.maximum(diff, 0.0)[None] * log_gamma[:, None, None]), 0.0)
    q_decay = jnp.exp((n + 1.0)[None] * log_gamma[:, None])
    k_decay = jnp.exp((C - 1.0 - n)[None] * log_gamma[:, None])
    chunk_decay = jnp.exp(C * log_gamma)
    qk_decay = jnp.stack([q_decay, k_decay], axis=1)[..., None]
    return jnp.cos(ang), jnp.sin(ang), decay_in, qk_decay, chunk_decay


def _retention_kernel(cd_ref, q_ref, k_ref, v_ref, g_ref, cos_ref, sin_ref, din_ref, qkd_ref, gn_ref,
                      o_ref, state_ref):
    c = pl.program_id(1)
    d = RET_HEAD_DIM
    half = d // 2

    @pl.when(c == 0)
    def _():
        state_ref[...] = jnp.zeros_like(state_ref)

    cos = cos_ref[...]
    sin = sin_ref[...]

    def rope(x):
        x1, x2 = x[:, :half], x[:, half:]
        return jnp.concatenate([x1 * cos - x2 * sin, x2 * cos + x1 * sin], axis=1)

    for h in range(RET_HEADS):
        sl = slice(h * d, (h + 1) * d)
        qr = rope(q_ref[:, sl].astype(F32))
        kr = rope(k_ref[:, sl].astype(F32)) * (d ** -0.5)
        v = v_ref[:, sl]
        qb = qr.astype(BF16)
        inner = lax.dot_general(qb, kr.astype(BF16), (((1,), (1,)), ((), ())), preferred_element_type=F32)
        inner = (inner * din_ref[h]).astype(BF16)
        state = state_ref[h]
        cross = jnp.dot((qr * qkd_ref[h, 0]).astype(BF16), state.astype(BF16), preferred_element_type=F32)
        o = jnp.dot(inner, v, preferred_element_type=F32) + cross
        kdt = (kr * qkd_ref[h, 1]).T.astype(BF16)
        state_ref[h] = state * cd_ref[h] + jnp.dot(kdt, v, preferred_element_type=F32)
        mu = jnp.mean(o, axis=-1, keepdims=True)
        var = jnp.mean(jnp.square(o - mu), axis=-1, keepdims=True)
        on = (o - mu) * lax.rsqrt(var + EPS) * gn_ref[:, sl]
        o_ref[:, sl] = (jax.nn.silu(g_ref[:, sl].astype(F32)) * on).astype(o_ref.dtype)


def retention(proj, gn_g, B, T):
    nc = T // RET_CHUNK
    cos, sin, decay_in, qk_decay, chunk_decay = _ret_tables(T)
    cblk = COL_RET // D_RET

    def seg(k):
        return pl.BlockSpec((RET_CHUNK, D_RET), lambda b, c: (b * nc + c, cblk + k))

    rope_spec = pl.BlockSpec((RET_CHUNK, RET_HEAD_DIM // 2), lambda b, c: (c, 0))
    return pl.pallas_call(
        _retention_kernel,
        grid=(B, nc),
        in_specs=[pl.BlockSpec(memory_space=pltpu.SMEM),
                  seg(0), seg(1), seg(2), seg(3), rope_spec, rope_spec,
                  pl.BlockSpec((RET_HEADS, RET_CHUNK, RET_CHUNK), lambda b, c: (0, 0, 0)),
                  pl.BlockSpec((RET_HEADS, 2, RET_CHUNK, 1), lambda b, c: (0, 0, 0, 0)),
                  pl.BlockSpec((1, D_RET), lambda b, c: (0, 0))],
        out_specs=pl.BlockSpec((RET_CHUNK, D_RET), lambda b, c: (b * nc + c, 0)),
        out_shape=jax.ShapeDtypeStruct((B * T, D_RET), BF16),
        scratch_shapes=[pltpu.VMEM((RET_HEADS, RET_HEAD_DIM, RET_HEAD_DIM), F32)],
        compiler_params=_cparams(("parallel", "arbitrary")),
        name="retention",
    )(chunk_decay, proj, proj, proj, proj, cos, sin, decay_in, qk_decay, gn_g.reshape(1, D_RET))


def kernel(x, norm1_g, w_in, nsa_q_norm_g, nsa_k_norm_g, cmp_pos, cmp_w1, cmp_w2, ret_gn_g,
           w_up_nsa, w_up_ret, w_out, norm2_g, w_ff1, w_ff2, rel_bias):
    B, T, D = x.shape
    depth = w_in.shape[0]
    xf = x.reshape(B * T, D)
    tp, cbt, brow = bias_tables(rel_bias)
    w_in_t = jnp.swapaxes(w_in, 1, 2)
    h, rs = rmsnorm_split(xf, norm1_g[0])
    for l in range(depth):
        proj, w_out_b, w_upa_b, w_upr_b = matmul_wcast(
            h, rs, w_in_t, l, n_out=D_PROJ, transposed=True, n_plain=COL_RET // 1024, seg_start=N_REAL_A,
            sides=((w_out, 32), (w_up_nsa, 16), (w_up_ret, 16)), name="in_proj")
        ks, vst, kw, vwt = kv_prep(proj, nsa_k_norm_g[l], B, T)
        kc, vct = compress_kv(proj, cmp_pos[l], cmp_w1[l], cmp_w2[l], nsa_k_norm_g[l], B, T)
        o_a = nsa_attention(proj, nsa_q_norm_g[l], kc, vct, ks, vst, kw, vwt, tp, cbt, brow, B, T)
        o_r = retention(proj, ret_gn_g[l], B, T)
        y = merge_up_proj(o_a, w_upa_b, o_r, w_upr_b, proj)
        xf, h2, rs2 = matmul_residual(y, w_out_b, xf, norm_g=norm2_g[l], name="out_proj")
        a, w_ff2_b = matmul_wcast(h2, rs2, w_ff1, l, n_out=D_FF, relu2=True, sides=((w_ff2, 128),), name="ffn_up")
        if l + 1 < depth:
            xf, h, rs = matmul_residual(a, w_ff2_b, xf, norm_g=norm1_g[l + 1], name="ffn_down")
        else:
            xf = matmul_residual(a, w_ff2_b, xf, name="ffn_down")
    return xf.reshape(B, T, D)
```
